```python
import jax
import jax.numpy as jnp
from jax import lax
import numpy as np

D_MODEL = 2048
BATCH = 2
SEQ = 8192
DEPTH = 1
DEC_BATCH = 32
DEC_SEQ = 16
PAST_LEN = 1024

CHUNK = 64
Q_BLOCK = 128
SB_HEADS = 8
SB_HEAD_DIM = 128
SB_WIDTH = SB_HEADS * SB_HEAD_DIM
MLA_HEADS = 16
Q_LORA = 512
KV_LORA = 512
QK_NOPE_DIM = 128
ROPE_DIM = 64
V_HEAD_DIM = 128
MLA_WIDTH = MLA_HEADS * V_HEAD_DIM
MLA_SCALE = (QK_NOPE_DIM + ROPE_DIM) ** -0.5
SB_SCALE = SB_HEAD_DIM ** -0.5
ROPE_THETA = 10000.0
N_GROUPS = 4
EXPERTS_PER_GROUP = 8
N_EXPERTS = N_GROUPS * EXPERTS_PER_GROUP
TOP_K_IN_GROUP = 2
D_EXPERT = 512
PLE_DIM = 256
IN_COLS = 3 * SB_WIDTH + Q_LORA + KV_LORA + ROPE_DIM + 2 * D_MODEL
DEEPNORM_ALPHA = (2.0 * DEPTH) ** 0.25
DEEPNORM_BETA = (8.0 * DEPTH) ** -0.25
LN_EPS = 1e-5
RMS_EPS = 1e-6
NEG_INF = -1e30

kernel_name = "stickbreak_mla_hmoe_streaming_step"


def _layer_norm(x, g, b):
    xf = x.astype(jnp.float32)
    mu = jnp.mean(xf, axis=-1, keepdims=True)
    xc = xf - mu
    var = jnp.mean(jnp.square(xc), axis=-1, keepdims=True)
    return (xc * lax.rsqrt(var + LN_EPS) * g.astype(jnp.float32) + b.astype(jnp.float32)).astype(x.dtype)


def _rms_norm(x, g):
    xf = x.astype(jnp.float32)
    ms = jnp.mean(jnp.square(xf), axis=-1, keepdims=True)
    return (xf * lax.rsqrt(ms + RMS_EPS) * g.astype(jnp.float32)).astype(x.dtype)


def _rope(x, pos):
    inv_freq = 1.0 / (ROPE_THETA ** (jnp.arange(0, ROPE_DIM, 2, dtype=jnp.float32) / ROPE_DIM))
    ang = pos.astype(jnp.float32)[:, None] * inv_freq[None, :]
    ang = jnp.concatenate([ang, ang], axis=-1)
    shape = (1, pos.shape[0]) + (1,) * (x.ndim - 3) + (ROPE_DIM,)
    cos = jnp.cos(ang).reshape(shape).astype(x.dtype)
    sin = jnp.sin(ang).reshape(shape).astype(x.dtype)
    x1, x2 = jnp.split(x, 2, axis=-1)
    return x * cos + jnp.concatenate([-x2, x1], axis=-1) * sin


def _project_in(h, pos, w_in, q_a_norm, kv_a_norm, w_uq):
    B, T, _ = h.shape
    z = h @ w_in
    widths = [SB_WIDTH, SB_WIDTH, SB_WIDTH, Q_LORA, KV_LORA, ROPE_DIM, D_MODEL]
    offs, acc = [], 0
    for w in widths:
        acc += w
        offs.append(acc)
    sb_q, sb_k, sb_v, cq, ckv, k_rope, g_a, g_b = jnp.split(z, offs, axis=-1)
    hs = (B, T, SB_HEADS, SB_HEAD_DIM)
    q = (_rms_norm(cq, q_a_norm) @ w_uq).reshape(B, T, MLA_HEADS, QK_NOPE_DIM + ROPE_DIM)
    q_nope, q_rope = q[..., :QK_NOPE_DIM], q[..., QK_NOPE_DIM:]
    return (sb_q.reshape(hs), sb_k.reshape(hs), sb_v.reshape(hs),
            q_nope, _rope(q_rope, pos), _rms_norm(ckv, kv_a_norm), _rope(k_rope, pos),
            jax.nn.sigmoid(g_a), jax.nn.sigmoid(g_b))


def _stick_breaking(q, k, v, q_pos, k_pos):
    z = jnp.einsum('bqhd,bkhd->bhqk', q, k).astype(jnp.float32) * SB_SCALE
    mask = (k_pos[None, :] < q_pos[:, None])[None, None]
    log_beta = jax.nn.log_sigmoid(z)
    log_om = jnp.where(mask, jax.nn.log_sigmoid(-z), 0.0)
    rest = lax.cumsum(log_om, axis=3, reverse=True) - log_om
    a = jnp.where(mask, jnp.exp(log_beta + rest), 0.0)
    return jnp.einsum('bhqk,bkhd->bqhd', a.astype(v.dtype), v)


def _mla_kv_up(ckv, w_ukv):
    B, T, _ = ckv.shape
    kv = (ckv @ w_ukv).reshape(B, T, MLA_HEADS, QK_NOPE_DIM + V_HEAD_DIM)
    return kv[..., :QK_NOPE_DIM], kv[..., QK_NOPE_DIM:]


def _mla_attend(q_nope, q_rope, k_nope, k_rope, v, q_pos, k_pos):
    s = (jnp.einsum('bqhd,bkhd->bhqk', q_nope, k_nope)
         + jnp.einsum('bqhr,bkr->bhqk', q_rope, k_rope)).astype(jnp.float32) * MLA_SCALE
    mask = ((k_pos[None, :] // CHUNK) <= (q_pos[:, None] // CHUNK))[None, None]
    p = jax.nn.softmax(jnp.where(mask, s, NEG_INF), axis=-1)
    return jnp.einsum('bhqk,bkhd->bqhd', p.astype(v.dtype), v)


def _sweep_query_blocks(attend, q_parts, q_pos):
    T = q_pos.shape[0]
    nb = T // Q_BLOCK
    blocks = tuple(jnp.moveaxis(q.reshape((q.shape[0], nb, Q_BLOCK) + q.shape[2:]), 1, 0) for q in q_parts)
    out = lax.map(lambda a: attend(*a), blocks + (q_pos.reshape(nb, Q_BLOCK),))
    out = jnp.moveaxis(out, 0, 1)
    return out.reshape((out.shape[0], T) + out.shape[3:])


def _merge_branches(o_sb, o_mla, g_a, g_b, w_br_a, w_br_b, w_o):
    B, T = o_sb.shape[:2]
    br_a = o_sb.reshape(B, T, SB_WIDTH) @ w_br_a
    br_b = o_mla.reshape(B, T, MLA_WIDTH) @ w_br_b
    return (g_a * br_a + g_b * br_b) @ w_o


def _hier_moe(x, w_rg, w_re, w_gate, w_up, w_down):
    n = x.shape[0]
    g_prob = jax.nn.softmax((x @ w_rg).astype(jnp.float32), axis=-1)
    grp = jnp.argmax(g_prob, axis=-1)
    grp_oh = jax.nn.one_hot(grp, N_GROUPS, dtype=jnp.float32)
    p_grp = jnp.sum(g_prob * grp_oh, axis=-1)
    e_logits = (x @ w_re).astype(jnp.float32).reshape(n, N_GROUPS, EXPERTS_PER_GROUP)
    in_grp = jnp.einsum('nge,ng->ne', e_logits, grp_oh)
    top_val, top_idx = lax.top_k(in_grp, TOP_K_IN_GROUP)
    w = p_grp[:, None] * jax.nn.softmax(top_val, axis=-1)
    flat = grp[:, None] * EXPERTS_PER_GROUP + top_idx
    comb = jnp.einsum('nk,nke->ne', w, jax.nn.one_hot(flat, N_EXPERTS, dtype=jnp.float32)).astype(x.dtype)
    y = jnp.zeros_like(x)
    for e in range(N_EXPERTS):
        h = jax.nn.silu(x @ w_gate[e]) * (x @ w_up[e])
        y = y + comb[:, e:e + 1] * (h @ w_down[e])
    return y


def _finish_layer(x, mix, pe, ln1_g, ln1_b, ln2_g, ln2_b, w_rg, w_re, w_gate, w_up, w_down,
                  w_ple_proj, w_ple_gate):
    B, T, D = x.shape
    x = _layer_norm(DEEPNORM_ALPHA * x + mix, ln1_g, ln1_b)
    f = _hier_moe(x.reshape(B * T, D), w_rg, w_re, w_gate, w_up, w_down).reshape(B, T, D)
    x = _layer_norm(DEEPNORM_ALPHA * x + f, ln2_g, ln2_b)
    return x + jax.nn.sigmoid(x @ w_ple_gate) * (pe @ w_ple_proj)


def setup_inputs(seed: int = 0) -> dict:
    key = jax.random.key(seed)
    ks = jax.random.split(key, 32)

    def nrm(k, shape, scale):
        return jax.random.normal(k, shape, jnp.float32) * scale

    def gain(k, shape):
        return 1.0 + nrm(k, shape, 0.05)

    L = DEPTH
    return {
        "x_prompt": nrm(ks[0], (BATCH, SEQ, D_MODEL), 1.0),
        "x_sample": nrm(ks[1], (DEC_BATCH, DEC_SEQ, D_MODEL), 1.0),
        "cache_sb_k": nrm(ks[2], (L, DEC_BATCH, PAST_LEN, SB_HEADS, SB_HEAD_DIM), 1.0),
        "cache_sb_v": nrm(ks[3], (L, DEC_BATCH, PAST_LEN, SB_HEADS, SB_HEAD_DIM), 1.0),
        "cache_mla_ckv": nrm(ks[4], (L, DEC_BATCH, PAST_LEN, KV_LORA), 1.0),
        "cache_mla_krope": nrm(ks[5], (L, DEC_BATCH, PAST_LEN, ROPE_DIM), 1.0),
        "p_prompt": nrm(ks[6], (L, BATCH, SEQ, PLE_DIM), 1.0),
        "p_sample": nrm(ks[7], (L, DEC_BATCH, DEC_SEQ, PLE_DIM), 1.0),
        "w_in": nrm(ks[8], (L, D_MODEL, IN_COLS), D_MODEL ** -0.5),
        "q_a_norm": gain(ks[9], (L, Q_LORA)),
        "kv_a_norm": gain(ks[10], (L, KV_LORA)),
        "w_uq": nrm(ks[11], (L, Q_LORA, MLA_HEADS * (QK_NOPE_DIM + ROPE_DIM)), Q_LORA ** -0.5),
        "w_ukv": nrm(ks[12], (L, KV_LORA, MLA_HEADS * (QK_NOPE_DIM + V_HEAD_DIM)), KV_LORA ** -0.5),
        "w_br_a": nrm(ks[13], (L, SB_WIDTH, D_MODEL), SB_WIDTH ** -0.5),
        "w_br_b": nrm(ks[14], (L, MLA_WIDTH, D_MODEL), MLA_WIDTH ** -0.5),
        "w_o": nrm(ks[15], (L, D_MODEL, D_MODEL), DEEPNORM_BETA * D_MODEL ** -0.5),
        "ln1_g": gain(ks[16], (L, D_MODEL)),
        "ln1_b": nrm(ks[17], (L, D_MODEL), 0.02),
        "ln2_g": gain(ks[18], (L, D_MODEL)),
        "ln2_b": nrm(ks[19], (L, D_MODEL), 0.02),
        "w_router_group": nrm(ks[20], (L, D_MODEL, N_GROUPS), D_MODEL ** -0.5),
        "w_router_expert": nrm(ks[21], (L, D_MODEL, N_EXPERTS), D_MODEL ** -0.5),
        "w_exp_gate": nrm(ks[22], (L, N_EXPERTS, D_MODEL, D_EXPERT), D_MODEL ** -0.5),
        "w_exp_up": nrm(ks[23], (L, N_EXPERTS, D_MODEL, D_EXPERT), D_MODEL ** -0.5),
        "w_exp_down": nrm(ks[24], (L, N_EXPERTS, D_EXPERT, D_MODEL), DEEPNORM_BETA * D_EXPERT ** -0.5),
        "w_ple_proj": nrm(ks[25], (L, PLE_DIM, D_MODEL), PLE_DIM ** -0.5),
        "w_ple_gate": nrm(ks[26], (L, D_MODEL, D_MODEL), D_MODEL ** -0.5),
    }


def reference(x_prompt, x_sample, cache_sb_k, cache_sb_v, cache_mla_ckv, cache_mla_krope,
              p_prompt, p_sample, w_in, q_a_norm, kv_a_norm, w_uq, w_ukv, w_br_a, w_br_b, w_o,
              ln1_g, ln1_b, ln2_g, ln2_b, w_router_group, w_router_expert, w_exp_gate, w_exp_up,
              w_exp_down, w_ple_proj, w_ple_gate):
    seq = x_prompt.shape[1]
    dec = x_sample.shape[1]
    past = cache_sb_k.shape[2]
    pos_p = jnp.arange(seq, dtype=jnp.int32)
    pos_s = past + jnp.arange(dec, dtype=jnp.int32)
    pos_all = jnp.arange(past + dec, dtype=jnp.int32)

    xp, xs = x_prompt, x_sample
    sbk_p, sbv_p, ckv_p, kr_p = [], [], [], []
    sbk_s, sbv_s, ckv_s, kr_s = [], [], [], []
    for i in range(DEPTH):
        sq, sk, sv, qn, qr, ckv, kr, ga, gb = _project_in(xp, pos_p, w_in[i], q_a_norm[i], kv_a_norm[i], w_uq[i])
        o_sb = _sweep_query_blocks(lambda q, qp: _stick_breaking(q, sk, sv, qp, pos_p), (sq,), pos_p)
        kn, vv = _mla_kv_up(ckv, w_ukv[i])
        o_mla = _sweep_query_blocks(lambda a, b, qp: _mla_attend(a, b, kn, kr, vv, qp, pos_p), (qn, qr), pos_p)
        mix = _merge_branches(o_sb, o_mla, ga, gb, w_br_a[i], w_br_b[i], w_o[i])
        xp = _finish_layer(xp, mix, p_prompt[i], ln1_g[i], ln1_b[i], ln2_g[i], ln2_b[i],
                           w_router_group[i], w_router_expert[i], w_exp_gate[i], w_exp_up[i],
                           w_exp_down[i], w_ple_proj[i], w_ple_gate[i])
        sbk_p.append(sk)
        sbv_p.append(sv)
        ckv_p.append(ckv)
        kr_p.append(kr)

        sq, sk, sv, qn, qr, ckv, kr, ga, gb = _project_in(xs, pos_s, w_in[i], q_a_norm[i], kv_a_norm[i], w_uq[i])
        k_all = jnp.concatenate([cache_sb_k[i], sk], axis=1)
        v_all = jnp.concatenate([cache_sb_v[i], sv], axis=1)
        o_sb = _stick_breaking(sq, k_all, v_all, pos_s, pos_all)
        ckv_all = jnp.concatenate([cache_mla_ckv[i], ckv], axis=1)
        kr_all = jnp.concatenate([cache_mla_krope[i], kr], axis=1)
        kn, vv = _mla_kv_up(ckv_all, w_ukv[i])
        o_mla = _mla_attend(qn, qr, kn, kr_all, vv, pos_s, pos_all)
        mix = _merge_branches(o_sb, o_mla, ga, gb, w_br_a[i], w_br_b[i], w_o[i])
        xs = _finish_layer(xs, mix, p_sample[i], ln1_g[i], ln1_b[i], ln2_g[i], ln2_b[i],
                           w_router_group[i], w_router_expert[i], w_exp_gate[i], w_exp_up[i],
                           w_exp_down[i], w_ple_proj[i], w_ple_gate[i])
        sbk_s.append(sk)
        sbv_s.append(sv)
        ckv_s.append(ckv)
        kr_s.append(kr)

    return (xp, xs,
            jnp.stack(sbk_p, axis=0), jnp.stack(sbv_p, axis=0),
            jnp.stack(ckv_p, axis=0), jnp.stack(kr_p, axis=0),
            jnp.stack(sbk_s, axis=0), jnp.stack(sbv_s, axis=0),
            jnp.stack(ckv_s, axis=0), jnp.stack(kr_s, axis=0))
```

```python
import functools

import jax
import jax.numpy as jnp
from jax import lax
from jax.experimental import pallas as pl
from jax.experimental.pallas import tpu as pltpu

F32 = jnp.float32
BF16 = jnp.bfloat16

SB_HEADS = 8
SB_HEAD_DIM = 128
MLA_HEADS = 16
QK_NOPE_DIM = 128
ROPE_DIM = 64
V_HEAD_DIM = 128
QK_DIM = QK_NOPE_DIM + ROPE_DIM
CHUNK = 64
N_GROUPS = 4
EXPERTS_PER_GROUP = 8
N_EXPERTS = N_GROUPS * EXPERTS_PER_GROUP
ROPE_THETA = 10000.0
LN_EPS = 1e-5
RMS_EPS = 1e-6
NEG_INF = -1e30
MLA_SCALE = QK_DIM ** -0.5
SB_SCALE = SB_HEAD_DIM ** -0.5
SB_SKIP = 120.0

VMEM_LIMIT = 56 * 1024 * 1024
ROW_TILE = 256
LANES = 128


def _params(*sem):
    return pltpu.CompilerParams(dimension_semantics=sem, vmem_limit_bytes=VMEM_LIMIT)


def _dot(a, b):
    return jnp.dot(a, b, preferred_element_type=F32)


def _dot_nt(a, b):
    return lax.dot_general(a, b, (((1,), (1,)), ((), ())), preferred_element_type=F32)


def _rms(x, g):
    ms = jnp.mean(x * x, axis=-1, keepdims=True)
    return x * lax.rsqrt(ms + RMS_EPS) * g


def _layer_norm(x, g, b):
    mu = jnp.mean(x, axis=-1, keepdims=True)
    xc = x - mu
    var = jnp.mean(xc * xc, axis=-1, keepdims=True)
    return xc * lax.rsqrt(var + LN_EPS) * g + b


def _sb_proj_kernel(x_ref, w_ref, q_ref, k_ref, v_ref, kb_ref, vb_ref, *, width):
    xb = x_ref[...].astype(BF16)
    q = _dot(xb, w_ref[:, 0:width])
    q_ref[...] = (q * SB_SCALE).astype(BF16)
    k = _dot(xb, w_ref[:, width:2 * width])
    k_ref[...] = k
    kb_ref[...] = k.astype(BF16)
    v = _dot(xb, w_ref[:, 2 * width:3 * width])
    v_ref[...] = v
    vb_ref[...] = v.astype(BF16)


def _sb_proj(x, w_sb, tm=ROW_TILE):
    m, d = x.shape
    width = w_sb.shape[1] // 3
    row = lambda i: (i, 0)
    out_sds = lambda dt: jax.ShapeDtypeStruct((m, width), dt)
    return pl.pallas_call(
        functools.partial(_sb_proj_kernel, width=width),
        out_shape=(out_sds(BF16), out_sds(F32), out_sds(F32), out_sds(BF16), out_sds(BF16)),
        grid=(m // tm,),
        in_specs=[pl.BlockSpec((tm, d), row), pl.BlockSpec((d, 3 * width), lambda i: (0, 0))],
        out_specs=tuple(pl.BlockSpec((tm, width), row) for _ in range(5)),
        compiler_params=_params("parallel"),
        name="sb_proj",
    )(x, w_sb)


def _mla_q_proj_kernel(x_ref, wlat_ref, gq_ref, gkv_ref, wuq_ref, cos_ref, sin_ref,
                       q_ref, ckv_ref, kr_ref, *, q_lora, kv_lora):
    xb = x_ref[...].astype(BF16)
    cos = cos_ref[...]
    sin = sin_ref[...]
    ckv = _dot(xb, wlat_ref[:, q_lora:q_lora + kv_lora])
    ckv_ref[...] = _rms(ckv, gkv_ref[...])
    kr2 = _dot(xb, wlat_ref[:, q_lora + kv_lora:])
    kr_ref[...] = kr2[:, :ROPE_DIM] * cos[:, :ROPE_DIM] + kr2[:, ROPE_DIM:] * sin[:, :ROPE_DIM]
    cq = _dot(xb, wlat_ref[:, 0:q_lora])
    cqn = _rms(cq, gq_ref[...]).astype(BF16)
    n_nope = MLA_HEADS * QK_NOPE_DIM
    n_rope = MLA_HEADS * ROPE_DIM
    for h in range(MLA_HEADS):
        qn = _dot(cqn, wuq_ref[:, h * QK_NOPE_DIM:(h + 1) * QK_NOPE_DIM])
        q_ref[h, :, 0:QK_NOPE_DIM] = (qn * MLA_SCALE).astype(BF16)
    for hp in range(MLA_HEADS // 2):
        lo = n_nope + hp * LANES
        qr = _dot(cqn, wuq_ref[:, lo:lo + LANES])
        qrr = _dot(cqn, wuq_ref[:, lo + n_rope:lo + n_rope + LANES])
        rot = ((qr * cos + qrr * sin) * MLA_SCALE).astype(BF16)
        q_ref[2 * hp, :, QK_NOPE_DIM:QK_DIM] = rot[:, :ROPE_DIM]
        q_ref[2 * hp + 1, :, QK_NOPE_DIM:QK_DIM] = rot[:, ROPE_DIM:]


def _mla_q_proj(x, w_lat, gq, gkv, w_uq_all, cos, sin, tm=ROW_TILE):
    m, d = x.shape
    q_lora, kv_lora = gq.shape[1], gkv.shape[1]
    row = lambda i: (i, 0)
    const = lambda i: (0, 0)
    return pl.pallas_call(
        functools.partial(_mla_q_proj_kernel, q_lora=q_lora, kv_lora=kv_lora),
        out_shape=(jax.ShapeDtypeStruct((MLA_HEADS, m, QK_DIM), BF16),
                   jax.ShapeDtypeStruct((m, kv_lora), F32),
                   jax.ShapeDtypeStruct((m, ROPE_DIM), F32)),
        grid=(m // tm,),
        in_specs=[pl.BlockSpec((tm, d), row),
                  pl.BlockSpec(w_lat.shape, const),
                  pl.BlockSpec((1, q_lora), const),
                  pl.BlockSpec((1, kv_lora), const),
                  pl.BlockSpec(w_uq_all.shape, const),
                  pl.BlockSpec((tm, LANES), row),
                  pl.BlockSpec((tm, LANES), row)],
        out_specs=(pl.BlockSpec((MLA_HEADS, tm, QK_DIM), lambda i: (0, i, 0)),
                   pl.BlockSpec((tm, kv_lora), row),
                   pl.BlockSpec((tm, ROPE_DIM), row)),
        compiler_params=_params("parallel"),
        name="mla_q_proj",
    )(x, w_lat, gq, gkv, w_uq_all, cos, sin)


def _mla_kv_proj_kernel(ckv_ref, kr_ref, wukv_ref, k_ref, v_ref):
    cb = ckv_ref[...].astype(BF16)
    krb = kr_ref[...].astype(BF16)
    n_k = MLA_HEADS * QK_NOPE_DIM
    for h in range(MLA_HEADS):
        kn = _dot(cb, wukv_ref[:, h * QK_NOPE_DIM:(h + 1) * QK_NOPE_DIM])
        k_ref[h, :, 0:QK_NOPE_DIM] = kn.astype(BF16)
        k_ref[h, :, QK_NOPE_DIM:QK_DIM] = krb
        vv = _dot(cb, wukv_ref[:, n_k + h * V_HEAD_DIM:n_k + (h + 1) * V_HEAD_DIM])
        v_ref[h] = vv.astype(BF16)


def _mla_kv_proj(ckv, kr, w_ukv_all, tm=ROW_TILE):
    m, kv_lora = ckv.shape
    row = lambda i: (i, 0)
    return pl.pallas_call(
        _mla_kv_proj_kernel,
        out_shape=(jax.ShapeDtypeStruct((MLA_HEADS, m, QK_DIM), BF16),
                   jax.ShapeDtypeStruct((MLA_HEADS, m, V_HEAD_DIM), BF16)),
        grid=(m // tm,),
        in_specs=[pl.BlockSpec((tm, kv_lora), row),
                  pl.BlockSpec((tm, ROPE_DIM), row),
                  pl.BlockSpec(w_ukv_all.shape, lambda i: (0, 0))],
        out_specs=(pl.BlockSpec((MLA_HEADS, tm, QK_DIM), lambda i: (0, i, 0)),
                   pl.BlockSpec((MLA_HEADS, tm, V_HEAD_DIM), lambda i: (0, i, 0))),
        compiler_params=_params("parallel"),
        name="mla_kv_proj",
    )(ckv, kr, w_ukv_all)


def _gates_kernel(x_ref, w_ref, o_ref):
    z = _dot(x_ref[...].astype(BF16), w_ref[...])
    o_ref[...] = 1.0 / (1.0 + jnp.exp(-z))


def _gates(x, w_g, tm=512, tn=1024):
    m, d = x.shape
    n = w_g.shape[1]
    return pl.pallas_call(
        _gates_kernel,
        out_shape=jax.ShapeDtypeStruct((m, n), F32),
        grid=(m // tm, n // tn),
        in_specs=[pl.BlockSpec((tm, d), lambda i, j: (i, 0)),
                  pl.BlockSpec((d, tn), lambda i, j: (0, j))],
        out_specs=pl.BlockSpec((tm, tn), lambda i, j: (i, j)),
        compiler_params=_params("parallel", "arbitrary"),
        name="gates",
    )(x, w_g)


def _sb_attn_kernel(q_ref, k_ref, vt_ref, o_ref, *, tq, tk, q_off):
    i = pl.program_id(2)
    t0 = q_off + i * tq
    q = q_ref[...]
    qpos = t0 + lax.broadcasted_iota(jnp.int32, (tk, tq), 1)
    krel = lax.broadcasted_iota(jnp.int32, (tk, tq), 0)
    r = lax.broadcasted_iota(jnp.int32, (tk, tk), 0)
    c = lax.broadcasted_iota(jnp.int32, (tk, tk), 1)
    upper = (c >= r).astype(BF16)

    def cond(carry):
        j, run, _ = carry
        return jnp.logical_and(j >= 0, jnp.min(run) < SB_SKIP)

    def body(carry):
        j, run, acc = carry
        k0 = pl.multiple_of(j * tk, tk)
        z = _dot_nt(k_ref[pl.ds(k0, tk), :], q)
        mask = (k0 + krel) < qpos
        sp = jnp.maximum(z, 0.0) + jnp.log(1.0 + jnp.exp(-jnp.abs(z)))
        sp = jnp.where(mask, sp, 0.0)
        hi = sp.astype(BF16)
        lo = (sp - hi.astype(F32)).astype(BF16)
        cs = _dot(upper, hi) + _dot(upper, lo) + run
        a = jnp.where(mask, jnp.exp(z - cs), 0.0)
        acc = acc + _dot(vt_ref[j], a.astype(BF16))
        return j - 1, cs[0:1, :], acc

    j_start = (t0 + tq - 2) // tk
    init = (j_start, jnp.zeros((1, tq), F32), jnp.zeros((SB_HEAD_DIM, tq), F32))
    _, _, acc = lax.while_loop(cond, body, init)
    o_ref[...] = acc.T.astype(o_ref.dtype)


def _sb_attn(q, k, vt, *, n_batch, tq_total, q_off, tq, tk):
    tk_total = k.shape[0] // n_batch
    nq = tq_total // tq
    hd = SB_HEAD_DIM
    return pl.pallas_call(
        functools.partial(_sb_attn_kernel, tq=tq, tk=tk, q_off=q_off),
        out_shape=jax.ShapeDtypeStruct((n_batch * tq_total, SB_HEADS * hd), BF16),
        grid=(n_batch, SB_HEADS, nq),
        in_specs=[pl.BlockSpec((tq, hd), lambda b, h, i: (b * nq + i, h)),
                  pl.BlockSpec((tk_total, hd), lambda b, h, i: (b, h)),
                  pl.BlockSpec((None, None, tk_total // tk, hd, tk), lambda b, h, i: (b, h, 0, 0, 0))],
        out_specs=pl.BlockSpec((tq, hd), lambda b, h, i: (b * nq + i, h)),
        compiler_params=_params("parallel", "parallel", "arbitrary"),
        name="sb_attn",
    )(q, k, vt)


def _mla_attn_kernel(q_ref, k_ref, v_ref, o_ref, *, tq, tk, q_off, kv_len):
    i = pl.program_id(2)
    t0 = q_off + i * tq
    q = q_ref[...]
    qchunk = (t0 + lax.broadcasted_iota(jnp.int32, (tq, tk), 0)) // CHUNK
    kcol = lax.broadcasted_iota(jnp.int32, (tq, tk), 1)

    def step(j, carry, masked):
        m, l, acc = carry
        k0 = pl.multiple_of(j * tk, tk)
        s = _dot_nt(q, k_ref[pl.ds(k0, tk), :])
        if masked:
            kidx = k0 + kcol
            ok = jnp.logical_and(kidx // CHUNK <= qchunk, kidx < kv_len)
            s = jnp.where(ok, s, NEG_INF)
        m_new = jnp.maximum(m, jnp.max(s, axis=1, keepdims=True))
        p = jnp.exp(s - m_new)
        alpha = jnp.exp(m - m_new)
        l = alpha * l + jnp.sum(p, axis=1, keepdims=True)
        acc = alpha * acc + _dot(p.astype(BF16), v_ref[pl.ds(k0, tk), :])
        return m_new, l, acc

    n_full = t0 // tk
    last_visible = jnp.minimum(((t0 + tq - 1) // CHUNK + 1) * CHUNK, kv_len) - 1
    n_all = last_visible // tk + 1
    init = (jnp.full((tq, 1), NEG_INF, F32), jnp.zeros((tq, 1), F32), jnp.zeros((tq, V_HEAD_DIM), F32))
    carry = lax.fori_loop(0, n_full, functools.partial(step, masked=False), init)
    _, l, acc = lax.fori_loop(n_full, n_all, functools.partial(step, masked=True), carry)
    o_ref[...] = (acc / l).astype(o_ref.dtype)


def _mla_attn(q_cat, k_cat, v, *, n_batch, q_blk0, tq_total, k_blk0, tk_total, kv_len, q_off, tq, tk):
    nq = tq_total // tq
    return pl.pallas_call(
        functools.partial(_mla_attn_kernel, tq=tq, tk=tk, q_off=q_off, kv_len=kv_len),
        out_shape=jax.ShapeDtypeStruct((n_batch * tq_total, MLA_HEADS * V_HEAD_DIM), BF16),
        grid=(n_batch, MLA_HEADS, nq),
        in_specs=[pl.BlockSpec((None, tq, QK_DIM), lambda b, h, i: (h, (q_blk0 + b) * nq + i, 0)),
                  pl.BlockSpec((None, tk_total, QK_DIM), lambda b, h, i: (h, k_blk0 + b, 0)),
                  pl.BlockSpec((None, tk_total, V_HEAD_DIM), lambda b, h, i: (h, k_blk0 + b, 0))],
        out_specs=pl.BlockSpec((tq, V_HEAD_DIM), lambda b, h, i: (b * nq + i, h)),
        compiler_params=_params("parallel", "parallel", "arbitrary"),
        name="mla_attn",
    )(q_cat, k_cat, v)


def _merge_kernel(x_ref, osb_ref, omla_ref, ga_ref, gb_ref, wa_ref, wb_ref, wo_ref, g_ref, b_ref,
                  o_ref, acc_ref, *, alpha):
    j = pl.program_id(1)

    @pl.when(j == 0)
    def _():
        acc_ref[...] = jnp.zeros_like(acc_ref)

    u = ga_ref[...] * _dot(osb_ref[...], wa_ref[...]) + gb_ref[...] * _dot(omla_ref[...], wb_ref[...])
    acc_ref[...] += _dot(u.astype(BF16), wo_ref[...])

    @pl.when(j == pl.num_programs(1) - 1)
    def _():
        o_ref[...] = _layer_norm(alpha * x_ref[...] + acc_ref[...], g_ref[...], b_ref[...])


def _merge(x, o_sb, o_mla, gates, w_br_a, w_br_b, w_o, ln_g, ln_b, *, alpha, tm=512, tn=512):
    m, d = x.shape
    nj = d // tn
    return pl.pallas_call(
        functools.partial(_merge_kernel, alpha=alpha),
        out_shape=jax.ShapeDtypeStruct((m, d), F32),
        grid=(m // tm, nj),
        in_specs=[pl.BlockSpec((tm, d), lambda i, j: (i, 0)),
                  pl.BlockSpec((tm, o_sb.shape[1]), lambda i, j: (i, 0)),
                  pl.BlockSpec((tm, o_mla.shape[1]), lambda i, j: (i, 0)),
                  pl.BlockSpec((tm, tn), lambda i, j: (i, j)),
                  pl.BlockSpec((tm, tn), lambda i, j: (i, nj + j)),
                  pl.BlockSpec((w_br_a.shape[0], tn), lambda i, j: (0, j)),
                  pl.BlockSpec((w_br_b.shape[0], tn), lambda i, j: (0, j)),
                  pl.BlockSpec((tn, d), lambda i, j: (j, 0)),
                  pl.BlockSpec((1, d), lambda i, j: (0, 0)),
                  pl.BlockSpec((1, d), lambda i, j: (0, 0))],
        out_specs=pl.BlockSpec((tm, d), lambda i, j: (i, 0)),
        scratch_shapes=[pltpu.VMEM((tm, d), F32)],
        compiler_params=_params("parallel", "arbitrary"),
        name="merge_ln1",
    )(x, o_sb, o_mla, gates, gates, w_br_a, w_br_b, w_o, ln_g, ln_b)


def _router_kernel(x_ref, wh_ref, wl_ref, o_ref):
    x = x_ref[...]
    xh = x.astype(BF16)
    xl = (x - xh.astype(F32)).astype(BF16)
    logits = _dot(xh, wh_ref[...]) + (_dot(xh, wl_ref[...]) + _dot(xl, wh_ref[...]))
    lane = lax.broadcasted_iota(jnp.int32, logits.shape, 1)
    big = jnp.int32(LANES)
    is_grp = lane < N_GROUPS
    gl = jnp.where(is_grp, logits, -jnp.inf)
    gmax = jnp.max(gl, axis=1, keepdims=True)
    gsum = jnp.sum(jnp.where(is_grp, jnp.exp(gl - gmax), 0.0), axis=1, keepdims=True)
    p_grp = 1.0 / gsum
    grp = jnp.min(jnp.where(jnp.logical_and(is_grp, gl == gmax), lane, big), axis=1, keepdims=True)
    lo = N_GROUPS + grp * EXPERTS_PER_GROUP
    in_grp = jnp.logical_and(lane >= lo, lane < lo + EXPERTS_PER_GROUP)
    el = jnp.where(in_grp, logits, -jnp.inf)
    v1 = jnp.max(el, axis=1, keepdims=True)
    i1 = jnp.min(jnp.where(el == v1, lane, big), axis=1, keepdims=True)
    el2 = jnp.where(lane == i1, -jnp.inf, el)
    v2 = jnp.max(el2, axis=1, keepdims=True)
    i2 = jnp.min(jnp.where(el2 == v2, lane, big), axis=1, keepdims=True)
    e2w = jnp.exp(v2 - v1)
    w1 = p_grp / (1.0 + e2w)
    w2 = p_grp * e2w / (1.0 + e2w)
    out = jnp.where(lane == 0, (i1 - N_GROUPS).astype(F32),
                    jnp.where(lane == 1, (i2 - N_GROUPS).astype(F32),
                              jnp.where(lane == 2, w1, jnp.where(lane == 3, w2, 0.0))))
    o_ref[...] = out


def _router(x1, w_hi, w_lo, tm=ROW_TILE):
    m, d = x1.shape
    return pl.pallas_call(
        _router_kernel,
        out_shape=jax.ShapeDtypeStruct((m, LANES), F32),
        grid=(m // tm,),
        in_specs=[pl.BlockSpec((tm, d), lambda i: (i, 0)),
                  pl.BlockSpec((d, LANES), lambda i: (0, 0)),
                  pl.BlockSpec((d, LANES), lambda i: (0, 0))],
        out_specs=pl.BlockSpec((tm, LANES), lambda i: (i, 0)),
        compiler_params=_params("parallel"),
        name="router",
    )(x1, w_hi, w_lo)


def _row_gather_start(idx_ref, src_hbm, dst_ref, sem, n_rows):
    def body(r, _):
        tok = idx_ref[0, 0, r]
        pltpu.make_async_copy(src_hbm.at[pl.ds(tok, 1)], dst_ref.at[pl.ds(r, 1)], sem).start()
        return 0
    lax.fori_loop(0, n_rows, body, 0)


def _row_gather_wait(src_hbm, dst_ref, sem, n_rows):
    pltpu.make_async_copy(src_hbm.at[pl.ds(0, n_rows)], dst_ref, sem).wait()


def _moe_kernel(te_ref, nu_ref, idx_ref, idx_next_ref, x_hbm, wg_ref, wu_ref, wd_ref, rw_ref,
                y_ref, xbuf, sem, wgb, wub, wdb, *, tm):
    t = pl.program_id(0)
    n_used = nu_ref[0]
    slot = lax.rem(t, 2)

    @pl.when(t == 0)
    def _():
        _row_gather_start(idx_ref, x_hbm, xbuf.at[0], sem.at[0], tm)

    @pl.when(t + 1 < n_used)
    def _():
        _row_gather_start(idx_next_ref, x_hbm, xbuf.at[1 - slot], sem.at[1 - slot], tm)

    @pl.when(jnp.logical_or(t == 0, te_ref[t] != te_ref[jnp.maximum(t - 1, 0)]))
    def _():
        wgb[...] = wg_ref[...].astype(BF16)
        wub[...] = wu_ref[...].astype(BF16)
        wdb[...] = wd_ref[...].astype(BF16)

    @pl.when(t < n_used)
    def _():
        _row_gather_wait(x_hbm, xbuf.at[slot], sem.at[slot], tm)
        xb = xbuf[slot].astype(BF16)
        g = _dot(xb, wgb[...])
        u = _dot(xb, wub[...])
        h = (g / (1.0 + jnp.exp(-g))) * u
        y_ref[...] = _dot(h.astype(BF16), wdb[...]) * rw_ref[...]

    @pl.when(t >= n_used)
    def _():
        y_ref[...] = jnp.zeros_like(y_ref)


def _moe(tile_expert, n_used, row_token, x1, w_gate, w_up, w_down, row_w, *, tm):
    n_tiles = tile_expert.shape[0]
    d = x1.shape[1]
    f = w_gate.shape[2]
    idx3 = row_token.reshape(n_tiles, 1, tm)
    wmap = lambda t, te, nu: (te[t], 0, 0)
    grid_spec = pltpu.PrefetchScalarGridSpec(
        num_scalar_prefetch=2,
        grid=(n_tiles,),
        in_specs=[pl.BlockSpec((1, 1, tm), lambda t, te, nu: (t, 0, 0), memory_space=pltpu.SMEM),
                  pl.BlockSpec((1, 1, tm), lambda t, te, nu: (jnp.minimum(t + 1, n_tiles - 1), 0, 0),
                               memory_space=pltpu.SMEM),
                  pl.BlockSpec(memory_space=pl.ANY),
                  pl.BlockSpec((None, d, f), wmap),
                  pl.BlockSpec((None, d, f), wmap),
                  pl.BlockSpec((None, f, d), wmap),
                  pl.BlockSpec((tm, 1), lambda t, te, nu: (t, 0))],
        out_specs=pl.BlockSpec((tm, d), lambda t, te, nu: (t, 0)),
        scratch_shapes=[pltpu.VMEM((2, tm, d), F32),
                        pltpu.SemaphoreType.DMA((2,)),
                        pltpu.VMEM((d, f), BF16),
                        pltpu.VMEM((d, f), BF16),
                        pltpu.VMEM((f, d), BF16)],
    )
    return pl.pallas_call(
        functools.partial(_moe_kernel, tm=tm),
        out_shape=jax.ShapeDtypeStruct((n_tiles * tm, d), F32),
        grid_spec=grid_spec,
        compiler_params=_params("arbitrary"),
        name="moe_experts",
    )(tile_expert, n_used, idx3, idx3, x1, w_gate, w_up, w_down, row_w)


def _final_kernel(p1_ref, p2_ref, p1n_ref, p2n_ref, y_hbm, x1_ref, pe_ref, wpp_ref, wpg_ref, g_ref, b_ref,
                  o_ref, ybuf, sem, *, tm, alpha):
    t = pl.program_id(0)
    slot = lax.rem(t, 2)

    def start(a_ref, b_ref2, s):
        _row_gather_start(a_ref, y_hbm, ybuf.at[s, 0], sem.at[s], tm)
        _row_gather_start(b_ref2, y_hbm, ybuf.at[s, 1], sem.at[s], tm)

    @pl.when(t == 0)
    def _():
        start(p1_ref, p2_ref, 0)

    @pl.when(t + 1 < pl.num_programs(0))
    def _():
        start(p1n_ref, p2n_ref, 1 - slot)

    _row_gather_wait(y_hbm, ybuf.at[slot, 0], sem.at[slot], tm)
    _row_gather_wait(y_hbm, ybuf.at[slot, 1], sem.at[slot], tm)
    f = ybuf[slot, 0] + ybuf[slot, 1]
    x2 = _layer_norm(alpha * x1_ref[...] + f, g_ref[...], b_ref[...])
    gate = 1.0 / (1.0 + jnp.exp(-_dot(x2.astype(BF16), wpg_ref[...])))
    o_ref[...] = x2 + gate * _dot(pe_ref[...].astype(BF16), wpp_ref[...])


def _final(pos1, pos2, y_sorted, x1, pe, w_pp, w_pg, ln_g, ln_b, *, alpha, tm=ROW_TILE):
    m, d = x1.shape
    n_tiles = m // tm
    p1 = pos1.reshape(n_tiles, 1, tm)
    p2 = pos2.reshape(n_tiles, 1, tm)
    cur = lambda t: (t, 0, 0)
    nxt = lambda t: (jnp.minimum(t + 1, n_tiles - 1), 0, 0)
    smem = lambda im: pl.BlockSpec((1, 1, tm), im, memory_space=pltpu.SMEM)
    const = lambda t: (0, 0)
    return pl.pallas_call(
        functools.partial(_final_kernel, tm=tm, alpha=alpha),
        out_shape=jax.ShapeDtypeStruct((m, d), F32),
        grid=(n_tiles,),
        in_specs=[smem(cur), smem(cur), smem(nxt), smem(nxt),
                  pl.BlockSpec(memory_space=pl.ANY),
                  pl.BlockSpec((tm, d), lambda t: (t, 0)),
                  pl.BlockSpec((tm, pe.shape[1]), lambda t: (t, 0)),
                  pl.BlockSpec(w_pp.shape, const),
                  pl.BlockSpec(w_pg.shape, const),
                  pl.BlockSpec((1, d), const),
                  pl.BlockSpec((1, d), const)],
        out_specs=pl.BlockSpec((tm, d), lambda t: (t, 0)),
        scratch_shapes=[pltpu.VMEM((2, 2, tm, d), F32), pltpu.SemaphoreType.DMA((2,))],
        compiler_params=_params("arbitrary"),
        name="combine_ln2_ple",
    )(p1, p2, p1, p2, y_sorted, x1, pe, w_pp, w_pg, ln_g, ln_b)


def _rotate_half_cols(w):
    half = ROPE_DIM // 2
    return jnp.concatenate([-w[..., half:], w[..., :half]], axis=-1)


def _rope_tables(pos):
    inv_freq = 1.0 / (ROPE_THETA ** (jnp.arange(0, ROPE_DIM, 2, dtype=F32) / ROPE_DIM))
    ang = pos.astype(F32)[:, None] * inv_freq[None, :]
    ang = jnp.concatenate([ang, ang, ang, ang], axis=-1)
    return jnp.cos(ang), jnp.sin(ang)


def _sb_value_tiles(v, n_batch, tk):
    t = v.shape[0] // n_batch
    v = v.reshape(n_batch, t // tk, tk, SB_HEADS, SB_HEAD_DIM)
    return jnp.transpose(v, (0, 3, 1, 4, 2))


def _route_tables(route, tm):
    m = route.shape[0]
    e_pair = jnp.concatenate([route[:, 0], route[:, 1]]).astype(jnp.int32)
    w_pair = jnp.concatenate([route[:, 2], route[:, 3]])
    tok = jnp.arange(m, dtype=jnp.int32)
    tok_pair = jnp.concatenate([tok, tok])
    onehot = (e_pair[:, None] == jnp.arange(N_EXPERTS, dtype=jnp.int32)[None, :]).astype(jnp.int32)
    csum = jnp.cumsum(onehot, axis=0)
    rank = jnp.sum(csum * onehot, axis=1) - 1
    counts = csum[-1]
    padded = ((counts + tm - 1) // tm) * tm
    pad_end = jnp.cumsum(padded)
    pad_off = pad_end - padded
    pos = pad_off[e_pair] + rank
    n_rows = (2 * m // tm + N_EXPERTS) * tm
    row_token = jnp.zeros((n_rows,), jnp.int32).at[pos].set(tok_pair)
    row_w = jnp.zeros((n_rows,), F32).at[pos].set(w_pair)
    tile_start = jnp.arange(n_rows // tm, dtype=jnp.int32) * tm
    tile_expert = jnp.minimum(jnp.searchsorted(pad_end, tile_start, side="right"), N_EXPERTS - 1)
    n_used = (pad_end[-1] // tm).astype(jnp.int32).reshape(1)
    last_expert = tile_expert[jnp.maximum(n_used[0] - 1, 0)]
    tile_expert = jnp.where(tile_start // tm < n_used[0], tile_expert, last_expert).astype(jnp.int32)
    return tile_expert, n_used, row_token, row_w.reshape(n_rows, 1), pos[:m], pos[m:]


def kernel(x_prompt, x_sample, cache_sb_k, cache_sb_v, cache_mla_ckv, cache_mla_krope, p_prompt, p_sample,
           w_in, q_a_norm, kv_a_norm, w_uq, w_ukv, w_br_a, w_br_b, w_o, ln1_g, ln1_b, ln2_g, ln2_b,
           w_router_group, w_router_expert, w_exp_gate, w_exp_up, w_exp_down, w_ple_proj, w_ple_gate):
    depth = w_in.shape[0]
    assert depth == 1, "single trunk layer"
    nb, seq, d = x_prompt.shape
    nd, dec, _ = x_sample.shape
    past = cache_sb_k.shape[2]
    sbw = SB_HEADS * SB_HEAD_DIM
    q_lora = q_a_norm.shape[1]
    kv_lora = kv_a_norm.shape[1]
    alpha = (2.0 * depth) ** 0.25
    mp = nb * seq
    ms = nd * dec

    w0 = w_in[0]
    o1 = 3 * sbw
    o2 = o1 + q_lora
    o3 = o2 + kv_lora
    o4 = o3 + ROPE_DIM
    w_sb = w0[:, :o1].astype(BF16)
    w_kr = w0[:, o3:o4]
    w_lat = jnp.concatenate([w0[:, o1:o3], w_kr, _rotate_half_cols(w_kr)], axis=1).astype(BF16)
    w_g = w0[:, o4:].astype(BF16)
    wq = w_uq[0].reshape(q_lora, MLA_HEADS, QK_DIM)
    wq_r = wq[:, :, QK_NOPE_DIM:]
    w_uq_all = jnp.concatenate([wq[:, :, :QK_NOPE_DIM].reshape(q_lora, -1), wq_r.reshape(q_lora, -1),
                                _rotate_half_cols(wq_r).reshape(q_lora, -1)], axis=1).astype(BF16)
    wkv = w_ukv[0].reshape(kv_lora, MLA_HEADS, QK_NOPE_DIM + V_HEAD_DIM)
    w_ukv_all = jnp.concatenate([wkv[:, :, :QK_NOPE_DIM].reshape(kv_lora, -1),
                                 wkv[:, :, QK_NOPE_DIM:].reshape(kv_lora, -1)], axis=1).astype(BF16)
    w_r = jnp.concatenate([w_router_group[0], w_router_expert[0]], axis=1)
    w_r = jnp.pad(w_r, ((0, 0), (0, LANES - w_r.shape[1])))
    w_r_hi = w_r.astype(BF16)
    w_r_lo = (w_r - w_r_hi.astype(F32)).astype(BF16)

    x_all = jnp.concatenate([x_prompt.reshape(mp, d), x_sample.reshape(ms, d)], axis=0)
    pe_all = jnp.concatenate([p_prompt[0].reshape(mp, -1), p_sample[0].reshape(ms, -1)], axis=0)
    pos_all = jnp.concatenate([jnp.tile(jnp.arange(seq, dtype=jnp.int32), nb),
                               jnp.tile(past + jnp.arange(dec, dtype=jnp.int32), nd)])
    cos, sin = _rope_tables(pos_all)

    sbq, sbk, sbv, sbk_b, sbv_b = _sb_proj(x_all, w_sb)
    q_cat, ckv, kr = _mla_q_proj(x_all, w_lat, q_a_norm, kv_a_norm, w_uq_all, cos, sin)
    gates = _gates(x_all, w_g)

    tile = 256
    vt_p = _sb_value_tiles(sbv_b[:mp], nb, tile)
    o_sb_p = _sb_attn(sbq[:mp], sbk_b[:mp], vt_p, n_batch=nb, tq_total=seq, q_off=0, tq=tile, tk=tile)
    k_cat, v_mla = _mla_kv_proj(ckv, kr, w_ukv_all)
    o_mla_p = _mla_attn(q_cat, k_cat, v_mla, n_batch=nb, q_blk0=0, tq_total=seq, k_blk0=0, tk_total=seq,
                        kv_len=seq, q_off=0, tq=tile, tk=tile)

    kv_len = past + dec
    kpad = -(-kv_len // tile) * tile
    qpad = LANES

    def with_cache(cache, new, dt):
        new = new.reshape((nd, dec) + cache.shape[2:])
        pad = jnp.zeros((nd, kpad - kv_len) + cache.shape[2:], dt)
        return jnp.concatenate([cache.astype(dt), new.astype(dt), pad], axis=1)

    k_s = with_cache(cache_sb_k[0].reshape(nd, past, sbw), sbk_b[mp:], BF16).reshape(nd * kpad, sbw)
    v_s = with_cache(cache_sb_v[0].reshape(nd, past, sbw), sbv_b[mp:], BF16).reshape(nd * kpad, sbw)
    q_s = jnp.pad(sbq[mp:].reshape(nd, dec, sbw), ((0, 0), (0, qpad - dec), (0, 0))).reshape(nd * qpad, sbw)
    o_sb_s = _sb_attn(q_s, k_s, _sb_value_tiles(v_s, nd, tile), n_batch=nd, tq_total=qpad, q_off=past,
                      tq=qpad, tk=tile)
    o_sb_s = o_sb_s.reshape(nd, qpad, sbw)[:, :dec].reshape(ms, sbw)

    ckv_s = with_cache(cache_mla_ckv[0], ckv[mp:], F32).reshape(nd * kpad, kv_lora)
    kr_s = with_cache(cache_mla_krope[0], kr[mp:], F32).reshape(nd * kpad, ROPE_DIM)
    k_cat_s, v_mla_s = _mla_kv_proj(ckv_s, kr_s, w_ukv_all)
    o_mla_s = _mla_attn(q_cat, k_cat_s, v_mla_s, n_batch=nd, q_blk0=mp // dec, tq_total=dec, k_blk0=0,
                        tk_total=kpad, kv_len=kv_len, q_off=past, tq=dec, tk=tile)

    o_sb = jnp.concatenate([o_sb_p, o_sb_s], axis=0)
    o_mla = jnp.concatenate([o_mla_p, o_mla_s], axis=0)

    x1 = _merge(x_all, o_sb, o_mla, gates, w_br_a[0].astype(BF16), w_br_b[0].astype(BF16),
                w_o[0].astype(BF16), ln1_g, ln1_b, alpha=alpha)
    route = _router(x1, w_r_hi, w_r_lo)
    tile_expert, n_used, row_token, row_w, pos1, pos2 = _route_tables(route, ROW_TILE)
    y_sorted = _moe(tile_expert, n_used, row_token, x1, w_exp_gate[0], w_exp_up[0], w_exp_down[0], row_w,
                    tm=ROW_TILE)
    y = _final(pos1, pos2, y_sorted, x1, pe_all, w_ple_proj[0].astype(BF16), w_ple_gate[0].astype(BF16),
               ln2_g, ln2_b, alpha=alpha)

    hs = (SB_HEADS, SB_HEAD_DIM)
    return (y[:mp].reshape(nb, seq, d), y[mp:].reshape(nd, dec, d),
            sbk[:mp].reshape((1, nb, seq) + hs), sbv[:mp].reshape((1, nb, seq) + hs),
            ckv[:mp].reshape(1, nb, seq, kv_lora), kr[:mp].reshape(1, nb, seq, ROPE_DIM),
            sbk[mp:].reshape((1, nd, dec) + hs), sbv[mp:].reshape((1, nd, dec) + hs),
            ckv[mp:].reshape(1, nd, dec, kv_lora), kr[mp:].reshape(1, nd, dec, ROPE_DIM))
```

```python
import functools

import jax
import jax.numpy as jnp
from jax import lax
from jax.experimental import pallas as pl
from jax.experimental.pallas import tpu as pltpu

F32 = jnp.float32
BF16 = jnp.bfloat16

SB_HEADS = 8
SB_HEAD_DIM = 128
MLA_HEADS = 16
QK_NOPE_DIM = 128
ROPE_DIM = 64
V_HEAD_DIM = 128
QK_DIM = QK_NOPE_DIM + ROPE_DIM
CHUNK = 64
N_GROUPS = 4
EXPERTS_PER_GROUP = 8
N_EXPERTS = N_GROUPS * EXPERTS_PER_GROUP
ROPE_THETA = 10000.0
LN_EPS = 1e-5
RMS_EPS = 1e-6
NEG_INF = -1e30
MLA_SCALE = QK_DIM ** -0.5 * 1.4426950408889634
SB_SCALE = SB_HEAD_DIM ** -0.5
SB_SKIP = 120.0

VMEM_LIMIT = 56 * 1024 * 1024
ROW_TILE = 256
LANES = 128


def _params(*sem):
    return pltpu.CompilerParams(dimension_semantics=sem, vmem_limit_bytes=VMEM_LIMIT)


def _dot(a, b):
    return jnp.dot(a, b, preferred_element_type=F32)


def _dot_nt(a, b):
    return lax.dot_general(a, b, (((1,), (1,)), ((), ())), preferred_element_type=F32)


def _rms(x, g):
    ms = jnp.mean(x * x, axis=-1, keepdims=True)
    return x * lax.rsqrt(ms + RMS_EPS) * g


def _layer_norm(x, g, b):
    mu = jnp.mean(x, axis=-1, keepdims=True)
    xc = x - mu
    var = jnp.mean(xc * xc, axis=-1, keepdims=True)
    return xc * lax.rsqrt(var + LN_EPS) * g + b


def _sb_proj_kernel(x_ref, w_ref, q_ref, k_ref, v_ref, kb_ref, vb_ref, *, width):
    xb = x_ref[...].astype(BF16)
    q = _dot(xb, w_ref[:, 0:width])
    q_ref[...] = (q * SB_SCALE).astype(BF16)
    k = _dot(xb, w_ref[:, width:2 * width])
    k_ref[...] = k
    kb_ref[...] = k.astype(BF16)
    v = _dot(xb, w_ref[:, 2 * width:3 * width])
    v_ref[...] = v
    vb_ref[...] = v.astype(BF16)


def _sb_proj(x, w_sb, tm=ROW_TILE):
    m, d = x.shape
    width = w_sb.shape[1] // 3
    row = lambda i: (i, 0)
    out_sds = lambda dt: jax.ShapeDtypeStruct((m, width), dt)
    return pl.pallas_call(
        functools.partial(_sb_proj_kernel, width=width),
        out_shape=(out_sds(BF16), out_sds(F32), out_sds(F32), out_sds(BF16), out_sds(BF16)),
        grid=(m // tm,),
        in_specs=[pl.BlockSpec((tm, d), row), pl.BlockSpec((d, 3 * width), lambda i: (0, 0))],
        out_specs=tuple(pl.BlockSpec((tm, width), row) for _ in range(5)),
        compiler_params=_params("parallel"),
        name="sb_proj",
    )(x, w_sb)


def _mla_q_proj_kernel(x_ref, wlat_ref, gq_ref, gkv_ref, wuq_ref, cos_ref, sin_ref,
                       q_ref, ckv_ref, kr_ref, *, q_lora, kv_lora):
    xb = x_ref[...].astype(BF16)
    cos = cos_ref[...]
    sin = sin_ref[...]
    ckv = _dot(xb, wlat_ref[:, q_lora:q_lora + kv_lora])
    ckv_ref[...] = _rms(ckv, gkv_ref[...])
    kr2 = _dot(xb, wlat_ref[:, q_lora + kv_lora:])
    kr_ref[...] = kr2[:, :ROPE_DIM] * cos[:, :ROPE_DIM] + kr2[:, ROPE_DIM:] * sin[:, :ROPE_DIM]
    cq = _dot(xb, wlat_ref[:, 0:q_lora])
    cqn = _rms(cq, gq_ref[...]).astype(BF16)
    n_nope = MLA_HEADS * QK_NOPE_DIM
    n_rope = MLA_HEADS * ROPE_DIM
    for h in range(MLA_HEADS):
        qn = _dot(cqn, wuq_ref[:, h * QK_NOPE_DIM:(h + 1) * QK_NOPE_DIM])
        q_ref[h, :, 0:QK_NOPE_DIM] = (qn * MLA_SCALE).astype(BF16)
    for hp in range(MLA_HEADS // 2):
        lo = n_nope + hp * LANES
        qr = _dot(cqn, wuq_ref[:, lo:lo + LANES])
        qrr = _dot(cqn, wuq_ref[:, lo + n_rope:lo + n_rope + LANES])
        rot = ((qr * cos + qrr * sin) * MLA_SCALE).astype(BF16)
        q_ref[2 * hp, :, QK_NOPE_DIM:QK_DIM] = rot[:, :ROPE_DIM]
        q_ref[2 * hp + 1, :, QK_NOPE_DIM:QK_DIM] = rot[:, ROPE_DIM:]


def _mla_q_proj(x, w_lat, gq, gkv, w_uq_all, cos, sin, tm=ROW_TILE):
    m, d = x.shape
    q_lora, kv_lora = gq.shape[1], gkv.shape[1]
    row = lambda i: (i, 0)
    const = lambda i: (0, 0)
    return pl.pallas_call(
        functools.partial(_mla_q_proj_kernel, q_lora=q_lora, kv_lora=kv_lora),
        out_shape=(jax.ShapeDtypeStruct((MLA_HEADS, m, QK_DIM), BF16),
                   jax.ShapeDtypeStruct((m, kv_lora), F32),
                   jax.ShapeDtypeStruct((m, ROPE_DIM), F32)),
        grid=(m // tm,),
        in_specs=[pl.BlockSpec((tm, d), row),
                  pl.BlockSpec(w_lat.shape, const),
                  pl.BlockSpec((1, q_lora), const),
                  pl.BlockSpec((1, kv_lora), const),
                  pl.BlockSpec(w_uq_all.shape, const),
                  pl.BlockSpec((tm, LANES), row),
                  pl.BlockSpec((tm, LANES), row)],
        out_specs=(pl.BlockSpec((MLA_HEADS, tm, QK_DIM), lambda i: (0, i, 0)),
                   pl.BlockSpec((tm, kv_lora), row),
                   pl.BlockSpec((tm, ROPE_DIM), row)),
        compiler_params=_params("parallel"),
        name="mla_q_proj",
    )(x, w_lat, gq, gkv, w_uq_all, cos, sin)


def _mla_kv_proj_kernel(ckv_ref, kr_ref, wukv_ref, kt_ref, v_ref):
    cb = ckv_ref[...].astype(BF16)
    kr = kr_ref[...]
    kr_t = jnp.concatenate([kr, kr], axis=1).T[0:ROPE_DIM, :].astype(BF16)
    n_k = MLA_HEADS * QK_NOPE_DIM
    for h in range(MLA_HEADS):
        kn = _dot(cb, wukv_ref[:, h * QK_NOPE_DIM:(h + 1) * QK_NOPE_DIM])
        kt_ref[h, 0:QK_NOPE_DIM, :] = kn.T.astype(BF16)
        kt_ref[h, QK_NOPE_DIM:QK_DIM, :] = kr_t
        vv = _dot(cb, wukv_ref[:, n_k + h * V_HEAD_DIM:n_k + (h + 1) * V_HEAD_DIM])
        v_ref[h] = vv.astype(BF16)


def _mla_kv_proj(ckv, kr, w_ukv_all, tm):
    m, kv_lora = ckv.shape
    row = lambda i: (i, 0)
    return pl.pallas_call(
        _mla_kv_proj_kernel,
        out_shape=(jax.ShapeDtypeStruct((MLA_HEADS, m // tm, QK_DIM, tm), BF16),
                   jax.ShapeDtypeStruct((MLA_HEADS, m, V_HEAD_DIM), BF16)),
        grid=(m // tm,),
        in_specs=[pl.BlockSpec((tm, kv_lora), row),
                  pl.BlockSpec((tm, ROPE_DIM), row),
                  pl.BlockSpec(w_ukv_all.shape, lambda i: (0, 0))],
        out_specs=(pl.BlockSpec((MLA_HEADS, None, QK_DIM, tm), lambda i: (0, i, 0, 0)),
                   pl.BlockSpec((MLA_HEADS, tm, V_HEAD_DIM), lambda i: (0, i, 0))),
        compiler_params=_params("parallel"),
        name="mla_kv_proj",
    )(ckv, kr, w_ukv_all)


def _gates_kernel(x_ref, w_ref, o_ref):
    z = _dot(x_ref[...].astype(BF16), w_ref[...])
    o_ref[...] = 1.0 / (1.0 + jnp.exp(-z))


def _gates(x, w_g, tm=512, tn=1024):
    m, d = x.shape
    n = w_g.shape[1]
    return pl.pallas_call(
        _gates_kernel,
        out_shape=jax.ShapeDtypeStruct((m, n), F32),
        grid=(m // tm, n // tn),
        in_specs=[pl.BlockSpec((tm, d), lambda i, j: (i, 0)),
                  pl.BlockSpec((d, tn), lambda i, j: (0, j))],
        out_specs=pl.BlockSpec((tm, tn), lambda i, j: (i, j)),
        compiler_params=_params("parallel", "arbitrary"),
        name="gates",
    )(x, w_g)


def _sb_attn_kernel(q_ref, k_ref, vt_ref, o_ref, *, tq, tk, q_off, n_heads):
    i = pl.program_id(2)
    t0 = q_off + i * tq
    hd = SB_HEAD_DIM
    heads = range(n_heads)
    qs = [q_ref[:, h * hd:(h + 1) * hd] for h in heads]
    rel = lax.broadcasted_iota(jnp.int32, (tk, tq), 0) - lax.broadcasted_iota(jnp.int32, (tk, tq), 1)
    r = lax.broadcasted_iota(jnp.int32, (tk, tk), 0)
    c = lax.broadcasted_iota(jnp.int32, (tk, tk), 1)
    upper = (c >= r).astype(BF16)

    def cond(carry):
        j, runs, _ = carry
        lowest = functools.reduce(jnp.minimum, runs)
        return jnp.logical_and(j >= 0, jnp.min(lowest) < SB_SKIP)

    def body(carry):
        j, runs, accs = carry
        k0 = pl.multiple_of(j * tk, tk)
        mask = rel < (t0 - k0)
        new_runs, new_accs = [], []
        for h in heads:
            z = _dot_nt(k_ref[pl.ds(k0, tk), h * hd:(h + 1) * hd], qs[h])
            sp = jnp.maximum(z, 0.0) + jnp.log(1.0 + jnp.exp(-jnp.abs(z)))
            sp = jnp.where(mask, sp, 0.0)
            hi = sp.astype(BF16)
            lo = (sp - hi.astype(F32)).astype(BF16)
            cs = _dot(upper, hi) + _dot(upper, lo) + runs[h]
            a = jnp.where(mask, jnp.exp(z - cs), 0.0)
            new_accs.append(accs[h] + _dot(vt_ref[h, j], a.astype(BF16)))
            new_runs.append(cs[0:1, :])
        return j - 1, tuple(new_runs), tuple(new_accs)

    j_start = (t0 + tq - 2) // tk
    init = (j_start, tuple(jnp.zeros((1, tq), F32) for _ in heads),
            tuple(jnp.zeros((hd, tq), F32) for _ in heads))
    _, _, accs = lax.while_loop(cond, body, init)
    for h in heads:
        o_ref[:, h * hd:(h + 1) * hd] = accs[h].T.astype(o_ref.dtype)


def _sb_attn(q, k, vt, *, n_batch, tq_total, q_off, tq, tk, n_heads=4):
    tk_total = k.shape[0] // n_batch
    nq = tq_total // tq
    hw = n_heads * SB_HEAD_DIM
    return pl.pallas_call(
        functools.partial(_sb_attn_kernel, tq=tq, tk=tk, q_off=q_off, n_heads=n_heads),
        out_shape=jax.ShapeDtypeStruct((n_batch * tq_total, SB_HEADS * SB_HEAD_DIM), BF16),
        grid=(n_batch, SB_HEADS // n_heads, nq),
        in_specs=[pl.BlockSpec((tq, hw), lambda b, g, i: (b * nq + i, g)),
                  pl.BlockSpec((tk_total, hw), lambda b, g, i: (b, g)),
                  pl.BlockSpec((None, n_heads, tk_total // tk, SB_HEAD_DIM, tk),
                               lambda b, g, i: (b, g, 0, 0, 0))],
        out_specs=pl.BlockSpec((tq, hw), lambda b, g, i: (b * nq + i, g)),
        compiler_params=_params("parallel", "parallel", "arbitrary"),
        name="sb_attn",
    )(q, k, vt)


def _mla_attn_kernel(q_ref, kt_ref, v_ref, o_ref, *, t):
    i = pl.program_id(2)
    q = q_ref[...]
    ones = jnp.ones((t, V_HEAD_DIM), BF16)

    def scores(j):
        return _dot(q, kt_ref[j])

    def consume(s, j, m, acc):
        v_ext = jnp.concatenate([v_ref[pl.ds(pl.multiple_of(j * t, t), t), :], ones], axis=1)
        m_new = jnp.maximum(m, jnp.max(s, axis=1, keepdims=True))
        p = jnp.exp2(s - m_new).astype(BF16)
        return m_new, jnp.exp2(m - m_new) * acc + _dot(p, v_ext)

    def body(j, carry):
        m, acc, s = carry
        s_next = scores(j + 1)
        m, acc = consume(s, j, m, acc)
        return m, acc, s_next

    init = (jnp.full((t, 1), NEG_INF, F32), jnp.zeros((t, 2 * V_HEAD_DIM), F32), scores(0))
    m, acc, s = lax.fori_loop(0, i, body, init)
    qchunk = lax.broadcasted_iota(jnp.int32, (t, t), 0) // CHUNK
    kchunk = lax.broadcasted_iota(jnp.int32, (t, t), 1) // CHUNK
    _, acc = consume(jnp.where(kchunk <= qchunk, s, NEG_INF), i, m, acc)
    o_ref[...] = (acc[:, :V_HEAD_DIM] / acc[:, V_HEAD_DIM:]).astype(o_ref.dtype)


def _mla_attn(q_cat, kt, v, *, n_batch, seq):
    t = kt.shape[3]
    assert seq % t == 0 and t % CHUNK == 0
    nq = seq // t
    return pl.pallas_call(
        functools.partial(_mla_attn_kernel, t=t),
        out_shape=jax.ShapeDtypeStruct((n_batch * seq, MLA_HEADS * V_HEAD_DIM), BF16),
        grid=(n_batch, MLA_HEADS, nq),
        in_specs=[pl.BlockSpec((None, t, QK_DIM), lambda b, h, i: (h, b * nq + i, 0)),
                  pl.BlockSpec((None, nq, QK_DIM, t), lambda b, h, i: (h, b, 0, 0)),
                  pl.BlockSpec((None, seq, V_HEAD_DIM), lambda b, h, i: (h, b, 0))],
        out_specs=pl.BlockSpec((t, V_HEAD_DIM), lambda b, h, i: (b * nq + i, h)),
        compiler_params=_params("parallel", "parallel", "arbitrary"),
        name="mla_attn",
    )(q_cat, kt, v)


def _mla_decode_kernel(q_ref, wuk_ref, wuv_ref, cckv_ref, ckr_ref, nckv_ref, nkr_ref, o_ref, *, past, dec):
    rows = MLA_HEADS * dec
    ql = jnp.concatenate([_dot(q_ref[h, :, 0:QK_NOPE_DIM], wuk_ref[h]) for h in range(MLA_HEADS)], axis=0)
    ql = ql.astype(BF16)
    qr = jnp.concatenate([q_ref[h, :, QK_NOPE_DIM:QK_DIM] for h in range(MLA_HEADS)], axis=0)
    kc = cckv_ref[...].astype(BF16)
    kn = nckv_ref[...].astype(BF16)
    s_c = _dot_nt(ql, kc) + _dot_nt(qr, ckr_ref[...].astype(BF16))
    s_n = _dot_nt(ql, kn) + _dot_nt(qr, nkr_ref[...].astype(BF16))
    qpos = past + lax.rem(lax.broadcasted_iota(jnp.int32, (rows, dec), 0), dec)
    kpos = past + lax.broadcasted_iota(jnp.int32, (rows, dec), 1)
    s_n = jnp.where(kpos // CHUNK <= qpos // CHUNK, s_n, NEG_INF)
    m = jnp.maximum(jnp.max(s_c, axis=1, keepdims=True), jnp.max(s_n, axis=1, keepdims=True))
    p_c = jnp.exp2(s_c - m)
    p_n = jnp.exp2(s_n - m)
    l = jnp.sum(p_c, axis=1, keepdims=True) + jnp.sum(p_n, axis=1, keepdims=True)
    ol = (_dot(p_c.astype(BF16), kc) + _dot(p_n.astype(BF16), kn)) / l
    ol = ol.astype(BF16)
    for h in range(MLA_HEADS):
        o_h = _dot(ol[h * dec:(h + 1) * dec, :], wuv_ref[h])
        o_ref[:, h * V_HEAD_DIM:(h + 1) * V_HEAD_DIM] = o_h.astype(o_ref.dtype)


def _mla_decode(q_cat, w_uk, w_uv, cache_ckv, cache_kr, ckv, kr, *, q_blk0, dec):
    nd, past, kv_lora = cache_ckv.shape
    const3 = lambda b: (0, 0, 0)
    return pl.pallas_call(
        functools.partial(_mla_decode_kernel, past=past, dec=dec),
        out_shape=jax.ShapeDtypeStruct((nd * dec, MLA_HEADS * V_HEAD_DIM), BF16),
        grid=(nd,),
        in_specs=[pl.BlockSpec((MLA_HEADS, dec, QK_DIM), lambda b: (0, q_blk0 + b, 0)),
                  pl.BlockSpec(w_uk.shape, const3),
                  pl.BlockSpec(w_uv.shape, const3),
                  pl.BlockSpec((None, past, kv_lora), lambda b: (b, 0, 0)),
                  pl.BlockSpec((None, past, ROPE_DIM), lambda b: (b, 0, 0)),
                  pl.BlockSpec((dec, kv_lora), lambda b: (q_blk0 + b, 0)),
                  pl.BlockSpec((dec, ROPE_DIM), lambda b: (q_blk0 + b, 0))],
        out_specs=pl.BlockSpec((dec, MLA_HEADS * V_HEAD_DIM), lambda b: (b, 0)),
        compiler_params=_params("parallel"),
        name="mla_decode",
    )(q_cat, w_uk, w_uv, cache_ckv, cache_kr, ckv, kr)


def _merge_kernel(x_ref, osb_ref, omla_ref, ga_ref, gb_ref, wa_ref, wb_ref, wo_ref, g_ref, b_ref,
                  o_ref, acc_ref, *, alpha):
    j = pl.program_id(1)

    @pl.when(j == 0)
    def _():
        acc_ref[...] = jnp.zeros_like(acc_ref)

    u = ga_ref[...] * _dot(osb_ref[...], wa_ref[...]) + gb_ref[...] * _dot(omla_ref[...], wb_ref[...])
    acc_ref[...] += _dot(u.astype(BF16), wo_ref[...])

    @pl.when(j == pl.num_programs(1) - 1)
    def _():
        o_ref[...] = _layer_norm(alpha * x_ref[...] + acc_ref[...], g_ref[...], b_ref[...])


def _merge(x, o_sb, o_mla, gates, w_br_a, w_br_b, w_o, ln_g, ln_b, *, alpha, tm=512, tn=512):
    m, d = x.shape
    nj = d // tn
    return pl.pallas_call(
        functools.partial(_merge_kernel, alpha=alpha),
        out_shape=jax.ShapeDtypeStruct((m, d), F32),
        grid=(m // tm, nj),
        in_specs=[pl.BlockSpec((tm, d), lambda i, j: (i, 0)),
                  pl.BlockSpec((tm, o_sb.shape[1]), lambda i, j: (i, 0)),
                  pl.BlockSpec((tm, o_mla.shape[1]), lambda i, j: (i, 0)),
                  pl.BlockSpec((tm, tn), lambda i, j: (i, j)),
                  pl.BlockSpec((tm, tn), lambda i, j: (i, nj + j)),
                  pl.BlockSpec((w_br_a.shape[0], tn), lambda i, j: (0, j)),
                  pl.BlockSpec((w_br_b.shape[0], tn), lambda i, j: (0, j)),
                  pl.BlockSpec((tn, d), lambda i, j: (j, 0)),
                  pl.BlockSpec((1, d), lambda i, j: (0, 0)),
                  pl.BlockSpec((1, d), lambda i, j: (0, 0))],
        out_specs=pl.BlockSpec((tm, d), lambda i, j: (i, 0)),
        scratch_shapes=[pltpu.VMEM((tm, d), F32)],
        compiler_params=_params("parallel", "arbitrary"),
        name="merge_ln1",
    )(x, o_sb, o_mla, gates, gates, w_br_a, w_br_b, w_o, ln_g, ln_b)


def _router_kernel(x_ref, wh_ref, wl_ref, o_ref):
    x = x_ref[...]
    xh = x.astype(BF16)
    xl = (x - xh.astype(F32)).astype(BF16)
    logits = _dot(xh, wh_ref[...]) + (_dot(xh, wl_ref[...]) + _dot(xl, wh_ref[...]))
    lane = lax.broadcasted_iota(jnp.int32, logits.shape, 1)
    big = jnp.int32(LANES)
    is_grp = lane < N_GROUPS
    gl = jnp.where(is_grp, logits, -jnp.inf)
    gmax = jnp.max(gl, axis=1, keepdims=True)
    gsum = jnp.sum(jnp.where(is_grp, jnp.exp(gl - gmax), 0.0), axis=1, keepdims=True)
    p_grp = 1.0 / gsum
    grp = jnp.min(jnp.where(jnp.logical_and(is_grp, gl == gmax), lane, big), axis=1, keepdims=True)
    lo = N_GROUPS + grp * EXPERTS_PER_GROUP
    in_grp = jnp.logical_and(lane >= lo, lane < lo + EXPERTS_PER_GROUP)
    el = jnp.where(in_grp, logits, -jnp.inf)
    v1 = jnp.max(el, axis=1, keepdims=True)
    i1 = jnp.min(jnp.where(el == v1, lane, big), axis=1, keepdims=True)
    el2 = jnp.where(lane == i1, -jnp.inf, el)
    v2 = jnp.max(el2, axis=1, keepdims=True)
    i2 = jnp.min(jnp.where(el2 == v2, lane, big), axis=1, keepdims=True)
    e2w = jnp.exp(v2 - v1)
    w1 = p_grp / (1.0 + e2w)
    w2 = p_grp * e2w / (1.0 + e2w)
    out = jnp.where(lane == 0, (i1 - N_GROUPS).astype(F32),
                    jnp.where(lane == 1, (i2 - N_GROUPS).astype(F32),
                              jnp.where(lane == 2, w1, jnp.where(lane == 3, w2, 0.0))))
    o_ref[...] = out


def _router(x1, w_hi, w_lo, tm=ROW_TILE):
    m, d = x1.shape
    return pl.pallas_call(
        _router_kernel,
        out_shape=jax.ShapeDtypeStruct((m, LANES), F32),
        grid=(m // tm,),
        in_specs=[pl.BlockSpec((tm, d), lambda i: (i, 0)),
                  pl.BlockSpec((d, LANES), lambda i: (0, 0)),
                  pl.BlockSpec((d, LANES), lambda i: (0, 0))],
        out_specs=pl.BlockSpec((tm, LANES), lambda i: (i, 0)),
        compiler_params=_params("parallel"),
        name="router",
    )(x1, w_hi, w_lo)


def _row_gather_start(idx_ref, src_hbm, dst_ref, sem, n_rows):
    def body(r, _):
        tok = idx_ref[0, 0, r]
        pltpu.make_async_copy(src_hbm.at[pl.ds(tok, 1)], dst_ref.at[pl.ds(r, 1)], sem).start()
        return 0
    lax.fori_loop(0, n_rows, body, 0)


def _row_gather_wait(src_hbm, dst_ref, sem, n_rows):
    pltpu.make_async_copy(src_hbm.at[pl.ds(0, n_rows)], dst_ref, sem).wait()


def _moe_kernel(te_ref, nu_ref, idx_ref, idx_next_ref, x_hbm, wg_ref, wu_ref, wd_ref, rw_ref,
                y_ref, xbuf, sem, wgb, wub, wdb, *, tm):
    t = pl.program_id(0)
    n_used = nu_ref[0]
    slot = lax.rem(t, 2)

    @pl.when(t == 0)
    def _():
        _row_gather_start(idx_ref, x_hbm, xbuf.at[0], sem.at[0], tm)

    @pl.when(t + 1 < n_used)
    def _():
        _row_gather_start(idx_next_ref, x_hbm, xbuf.at[1 - slot], sem.at[1 - slot], tm)

    @pl.when(jnp.logical_or(t == 0, te_ref[t] != te_ref[jnp.maximum(t - 1, 0)]))
    def _():
        wgb[...] = wg_ref[...].astype(BF16)
        wub[...] = wu_ref[...].astype(BF16)
        wdb[...] = wd_ref[...].astype(BF16)

    @pl.when(t < n_used)
    def _():
        _row_gather_wait(x_hbm, xbuf.at[slot], sem.at[slot], tm)
        xb = xbuf[slot].astype(BF16)
        g = _dot(xb, wgb[...])
        u = _dot(xb, wub[...])
        h = (g / (1.0 + jnp.exp(-g))) * u
        y_ref[...] = _dot(h.astype(BF16), wdb[...]) * rw_ref[...]

    @pl.when(t >= n_used)
    def _():
        y_ref[...] = jnp.zeros_like(y_ref)


def _moe(tile_expert, n_used, row_token, x1, w_gate, w_up, w_down, row_w, *, tm):
    n_tiles = tile_expert.shape[0]
    d = x1.shape[1]
    f = w_gate.shape[2]
    idx3 = row_token.reshape(n_tiles, 1, tm)
    wmap = lambda t, te, nu: (te[t], 0, 0)
    grid_spec = pltpu.PrefetchScalarGridSpec(
        num_scalar_prefetch=2,
        grid=(n_tiles,),
        in_specs=[pl.BlockSpec((1, 1, tm), lambda t, te, nu: (t, 0, 0), memory_space=pltpu.SMEM),
                  pl.BlockSpec((1, 1, tm), lambda t, te, nu: (jnp.minimum(t + 1, n_tiles - 1), 0, 0),
                               memory_space=pltpu.SMEM),
                  pl.BlockSpec(memory_space=pl.ANY),
                  pl.BlockSpec((None, d, f), wmap),
                  pl.BlockSpec((None, d, f), wmap),
                  pl.BlockSpec((None, f, d), wmap),
                  pl.BlockSpec((tm, 1), lambda t, te, nu: (t, 0))],
        out_specs=pl.BlockSpec((tm, d), lambda t, te, nu: (t, 0)),
        scratch_shapes=[pltpu.VMEM((2, tm, d), F32),
                        pltpu.SemaphoreType.DMA((2,)),
                        pltpu.VMEM((d, f), BF16),
                        pltpu.VMEM((d, f), BF16),
                        pltpu.VMEM((f, d), BF16)],
    )
    return pl.pallas_call(
        functools.partial(_moe_kernel, tm=tm),
        out_shape=jax.ShapeDtypeStruct((n_tiles * tm, d), F32),
        grid_spec=grid_spec,
        compiler_params=_params("arbitrary"),
        name="moe_experts",
    )(tile_expert, n_used, idx3, idx3, x1, w_gate, w_up, w_down, row_w)


def _final_kernel(p1_ref, p2_ref, p1n_ref, p2n_ref, y_hbm, x1_ref, pe_ref, wpp_ref, wpg_ref, g_ref, b_ref,
                  o_ref, ybuf, sem, *, tm, alpha):
    t = pl.program_id(0)
    slot = lax.rem(t, 2)

    def start(a_ref, b_ref2, s):
        _row_gather_start(a_ref, y_hbm, ybuf.at[s, 0], sem.at[s], tm)
        _row_gather_start(b_ref2, y_hbm, ybuf.at[s, 1], sem.at[s], tm)

    @pl.when(t == 0)
    def _():
        start(p1_ref, p2_ref, 0)

    @pl.when(t + 1 < pl.num_programs(0))
    def _():
        start(p1n_ref, p2n_ref, 1 - slot)

    _row_gather_wait(y_hbm, ybuf.at[slot, 0], sem.at[slot], tm)
    _row_gather_wait(y_hbm, ybuf.at[slot, 1], sem.at[slot], tm)
    f = ybuf[slot, 0] + ybuf[slot, 1]
    x2 = _layer_norm(alpha * x1_ref[...] + f, g_ref[...], b_ref[...])
    gate = 1.0 / (1.0 + jnp.exp(-_dot(x2.astype(BF16), wpg_ref[...])))
    o_ref[...] = x2 + gate * _dot(pe_ref[...].astype(BF16), wpp_ref[...])


def _final(pos1, pos2, y_sorted, x1, pe, w_pp, w_pg, ln_g, ln_b, *, alpha, tm=ROW_TILE):
    m, d = x1.shape
    n_tiles = m // tm
    p1 = pos1.reshape(n_tiles, 1, tm)
    p2 = pos2.reshape(n_tiles, 1, tm)
    cur = lambda t: (t, 0, 0)
    nxt = lambda t: (jnp.minimum(t + 1, n_tiles - 1), 0, 0)
    smem = lambda im: pl.BlockSpec((1, 1, tm), im, memory_space=pltpu.SMEM)
    const = lambda t: (0, 0)
    return pl.pallas_call(
        functools.partial(_final_kernel, tm=tm, alpha=alpha),
        out_shape=jax.ShapeDtypeStruct((m, d), F32),
        grid=(n_tiles,),
        in_specs=[smem(cur), smem(cur), smem(nxt), smem(nxt),
                  pl.BlockSpec(memory_space=pl.ANY),
                  pl.BlockSpec((tm, d), lambda t: (t, 0)),
                  pl.BlockSpec((tm, pe.shape[1]), lambda t: (t, 0)),
                  pl.BlockSpec(w_pp.shape, const),
                  pl.BlockSpec(w_pg.shape, const),
                  pl.BlockSpec((1, d), const),
                  pl.BlockSpec((1, d), const)],
        out_specs=pl.BlockSpec((tm, d), lambda t: (t, 0)),
        scratch_shapes=[pltpu.VMEM((2, 2, tm, d), F32), pltpu.SemaphoreType.DMA((2,))],
        compiler_params=_params("arbitrary"),
        name="combine_ln2_ple",
    )(p1, p2, p1, p2, y_sorted, x1, pe, w_pp, w_pg, ln_g, ln_b)


def _rotate_half_cols(w):
    half = ROPE_DIM // 2
    return jnp.concatenate([-w[..., half:], w[..., :half]], axis=-1)


def _rope_tables(pos):
    inv_freq = 1.0 / (ROPE_THETA ** (jnp.arange(0, ROPE_DIM, 2, dtype=F32) / ROPE_DIM))
    ang = pos.astype(F32)[:, None] * inv_freq[None, :]
    ang = jnp.concatenate([ang, ang, ang, ang], axis=-1)
    return jnp.cos(ang), jnp.sin(ang)


def _sb_value_tiles(v, n_batch, tk):
    t = v.shape[0] // n_batch
    v = v.reshape(n_batch, t // tk, tk, SB_HEADS, SB_HEAD_DIM)
    return jnp.transpose(v, (0, 3, 1, 4, 2))


def _route_tables(route, tm):
    m = route.shape[0]
    e_pair = jnp.concatenate([route[:, 0], route[:, 1]]).astype(jnp.int32)
    w_pair = jnp.concatenate([route[:, 2], route[:, 3]])
    tok = jnp.arange(m, dtype=jnp.int32)
    tok_pair = jnp.concatenate([tok, tok])
    onehot = (e_pair[:, None] == jnp.arange(N_EXPERTS, dtype=jnp.int32)[None, :]).astype(jnp.int32)
    csum = jnp.cumsum(onehot, axis=0)
    rank = jnp.sum(csum * onehot, axis=1) - 1
    counts = csum[-1]
    padded = ((counts + tm - 1) // tm) * tm
    pad_end = jnp.cumsum(padded)
    pad_off = pad_end - padded
    pos = pad_off[e_pair] + rank
    n_rows = (2 * m // tm + N_EXPERTS) * tm
    row_token = jnp.zeros((n_rows,), jnp.int32).at[pos].set(tok_pair)
    row_w = jnp.zeros((n_rows,), F32).at[pos].set(w_pair)
    tile_start = jnp.arange(n_rows // tm, dtype=jnp.int32) * tm
    tile_expert = jnp.sum((pad_end[None, :] <= tile_start[:, None]).astype(jnp.int32), axis=1)
    tile_expert = jnp.minimum(tile_expert, N_EXPERTS - 1)
    n_used = (pad_end[-1] // tm).astype(jnp.int32).reshape(1)
    last_expert = tile_expert[jnp.maximum(n_used[0] - 1, 0)]
    tile_expert = jnp.where(tile_start // tm < n_used[0], tile_expert, last_expert).astype(jnp.int32)
    return tile_expert, n_used, row_token, row_w.reshape(n_rows, 1), pos[:m], pos[m:]


def kernel(x_prompt, x_sample, cache_sb_k, cache_sb_v, cache_mla_ckv, cache_mla_krope, p_prompt, p_sample,
           w_in, q_a_norm, kv_a_norm, w_uq, w_ukv, w_br_a, w_br_b, w_o, ln1_g, ln1_b, ln2_g, ln2_b,
           w_router_group, w_router_expert, w_exp_gate, w_exp_up, w_exp_down, w_ple_proj, w_ple_gate):
    depth = w_in.shape[0]
    assert depth == 1, "single trunk layer"
    nb, seq, d = x_prompt.shape
    nd, dec, _ = x_sample.shape
    past = cache_sb_k.shape[2]
    sbw = SB_HEADS * SB_HEAD_DIM
    q_lora = q_a_norm.shape[1]
    kv_lora = kv_a_norm.shape[1]
    alpha = (2.0 * depth) ** 0.25
    mp = nb * seq
    ms = nd * dec

    w0 = w_in[0]
    o1 = 3 * sbw
    o2 = o1 + q_lora
    o3 = o2 + kv_lora
    o4 = o3 + ROPE_DIM
    w_sb = w0[:, :o1].astype(BF16)
    w_kr = w0[:, o3:o4]
    w_lat = jnp.concatenate([w0[:, o1:o3], w_kr, _rotate_half_cols(w_kr)], axis=1).astype(BF16)
    w_g = w0[:, o4:].astype(BF16)
    wq = w_uq[0].reshape(q_lora, MLA_HEADS, QK_DIM)
    wq_r = wq[:, :, QK_NOPE_DIM:]
    w_uq_all = jnp.concatenate([wq[:, :, :QK_NOPE_DIM].reshape(q_lora, -1), wq_r.reshape(q_lora, -1),
                                _rotate_half_cols(wq_r).reshape(q_lora, -1)], axis=1).astype(BF16)
    wkv = w_ukv[0].reshape(kv_lora, MLA_HEADS, QK_NOPE_DIM + V_HEAD_DIM)
    w_ukv_all = jnp.concatenate([wkv[:, :, :QK_NOPE_DIM].reshape(kv_lora, -1),
                                 wkv[:, :, QK_NOPE_DIM:].reshape(kv_lora, -1)], axis=1).astype(BF16)
    w_r = jnp.concatenate([w_router_group[0], w_router_expert[0]], axis=1)
    w_r = jnp.pad(w_r, ((0, 0), (0, LANES - w_r.shape[1])))
    w_r_hi = w_r.astype(BF16)
    w_r_lo = (w_r - w_r_hi.astype(F32)).astype(BF16)

    x_all = jnp.concatenate([x_prompt.reshape(mp, d), x_sample.reshape(ms, d)], axis=0)
    pe_all = jnp.concatenate([p_prompt[0].reshape(mp, -1), p_sample[0].reshape(ms, -1)], axis=0)
    pos_all = jnp.concatenate([jnp.tile(jnp.arange(seq, dtype=jnp.int32), nb),
                               jnp.tile(past + jnp.arange(dec, dtype=jnp.int32), nd)])
    cos, sin = _rope_tables(pos_all)

    sbq, sbk, sbv, sbk_b, sbv_b = _sb_proj(x_all, w_sb)
    q_cat, ckv, kr = _mla_q_proj(x_all, w_lat, q_a_norm, kv_a_norm, w_uq_all, cos, sin)
    gates = _gates(x_all, w_g)

    tile = 256
    vt_p = _sb_value_tiles(sbv_b[:mp], nb, tile)
    o_sb_p = _sb_attn(sbq[:mp], sbk_b[:mp], vt_p, n_batch=nb, tq_total=seq, q_off=0, tq=tile, tk=tile)
    kt_mla, v_mla = _mla_kv_proj(ckv, kr, w_ukv_all, tm=512)
    o_mla_p = _mla_attn(q_cat, kt_mla, v_mla, n_batch=nb, seq=seq)

    kv_len = past + dec
    kpad = -(-kv_len // tile) * tile
    qpad = LANES

    def with_cache(cache, new, dt):
        new = new.reshape((nd, dec) + cache.shape[2:])
        pad = jnp.zeros((nd, kpad - kv_len) + cache.shape[2:], dt)
        return jnp.concatenate([cache.astype(dt), new.astype(dt), pad], axis=1)

    k_s = with_cache(cache_sb_k[0].reshape(nd, past, sbw), sbk_b[mp:], BF16).reshape(nd * kpad, sbw)
    v_s = with_cache(cache_sb_v[0].reshape(nd, past, sbw), sbv_b[mp:], BF16).reshape(nd * kpad, sbw)
    q_s = jnp.pad(sbq[mp:].reshape(nd, dec, sbw), ((0, 0), (0, qpad - dec), (0, 0))).reshape(nd * qpad, sbw)
    o_sb_s = _sb_attn(q_s, k_s, _sb_value_tiles(v_s, nd, tile), n_batch=nd, tq_total=qpad, q_off=past,
                      tq=qpad, tk=tile)
    o_sb_s = o_sb_s.reshape(nd, qpad, sbw)[:, :dec].reshape(ms, sbw)

    w_uk = jnp.transpose(wkv[:, :, :QK_NOPE_DIM], (1, 2, 0)).astype(BF16)
    w_uv = jnp.transpose(wkv[:, :, QK_NOPE_DIM:], (1, 0, 2)).astype(BF16)
    o_mla_s = _mla_decode(q_cat, w_uk, w_uv, cache_mla_ckv[0], cache_mla_krope[0], ckv, kr,
                          q_blk0=mp // dec, dec=dec)

    o_sb = jnp.concatenate([o_sb_p, o_sb_s], axis=0)
    o_mla = jnp.concatenate([o_mla_p, o_mla_s], axis=0)

    x1 = _merge(x_all, o_sb, o_mla, gates, w_br_a[0].astype(BF16), w_br_b[0].astype(BF16),
                w_o[0].astype(BF16), ln1_g, ln1_b, alpha=alpha)
    route = _router(x1, w_r_hi, w_r_lo)
    tile_expert, n_used, row_token, row_w, pos1, pos2 = _route_tables(route, ROW_TILE)
    y_sorted = _moe(tile_expert, n_used, row_token, x1, w_exp_gate[0], w_exp_up[0], w_exp_down[0], row_w,
                    tm=ROW_TILE)
    y = _final(pos1, pos2, y_sorted, x1, pe_all, w_ple_proj[0].astype(BF16), w_ple_gate[0].astype(BF16),
               ln2_g, ln2_b, alpha=alpha)

    hs = (SB_HEADS, SB_HEAD_DIM)
    return (y[:mp].reshape(nb, seq, d), y[mp:].reshape(nd, dec, d),
            sbk[:mp].reshape((1, nb, seq) + hs), sbv[:mp].reshape((1, nb, seq) + hs),
            ckv[:mp].reshape(1, nb, seq, kv_lora), kr[:mp].reshape(1, nb, seq, ROPE_DIM),
            sbk[mp:].reshape((1, nd, dec) + hs), sbv[mp:].reshape((1, nd, dec) + hs),
            ckv[mp:].reshape(1, nd, dec, kv_lora), kr[mp:].reshape(1, nd, dec, ROPE_DIM))
```

```python
import functools

import jax
import jax.numpy as jnp
from jax import lax
from jax.experimental import pallas as pl
from jax.experimental.pallas import tpu as pltpu

F32 = jnp.float32
BF16 = jnp.bfloat16

SB_HEADS = 8
SB_HEAD_DIM = 128
MLA_HEADS = 16
QK_NOPE_DIM = 128
ROPE_DIM = 64
V_HEAD_DIM = 128
QK_DIM = QK_NOPE_DIM + ROPE_DIM
CHUNK = 64
N_GROUPS = 4
EXPERTS_PER_GROUP = 8
N_EXPERTS = N_GROUPS * EXPERTS_PER_GROUP
ROPE_THETA = 10000.0
LN_EPS = 1e-5
RMS_EPS = 1e-6
NEG_INF = -1e30
MLA_SCALE = QK_DIM ** -0.5 * 1.4426950408889634
SB_SCALE = SB_HEAD_DIM ** -0.5
SB_SKIP = 120.0

VMEM_LIMIT = 56 * 1024 * 1024
ROW_TILE = 256
LANES = 128


def _params(*sem):
    return pltpu.CompilerParams(dimension_semantics=sem, vmem_limit_bytes=VMEM_LIMIT)


def _dot(a, b):
    return jnp.dot(a, b, preferred_element_type=F32)


def _dot_nt(a, b):
    return lax.dot_general(a, b, (((1,), (1,)), ((), ())), preferred_element_type=F32)


def _rms(x, g):
    ms = jnp.mean(x * x, axis=-1, keepdims=True)
    return x * lax.rsqrt(ms + RMS_EPS) * g


def _layer_norm(x, g, b):
    mu = jnp.mean(x, axis=-1, keepdims=True)
    xc = x - mu
    var = jnp.mean(xc * xc, axis=-1, keepdims=True)
    return xc * lax.rsqrt(var + LN_EPS) * g + b


def _sb_proj_kernel(x_ref, w_ref, q_ref, k_ref, v_ref, *tile_refs, width):
    xb = x_ref[...].astype(BF16)
    q = _dot(xb, w_ref[:, 0:width])
    q_ref[...] = (q * SB_SCALE).astype(BF16)
    k = _dot(xb, w_ref[:, width:2 * width])
    k_ref[...] = k
    v = _dot(xb, w_ref[:, 2 * width:3 * width])
    v_ref[...] = v
    if tile_refs:
        kb_ref, vt_ref = tile_refs
        kb_ref[...] = k.astype(BF16)
        for h in range(SB_HEADS):
            vt_ref[h] = v[:, h * SB_HEAD_DIM:(h + 1) * SB_HEAD_DIM].T.astype(BF16)


def _sb_proj(x, w_sb, *, seq=None, tm=ROW_TILE):
    m, d = x.shape
    width = w_sb.shape[1] // 3
    row = lambda i: (i, 0)
    out_sds = lambda dt: jax.ShapeDtypeStruct((m, width), dt)
    out_shape = [out_sds(BF16), out_sds(F32), out_sds(F32)]
    out_specs = [pl.BlockSpec((tm, width), row) for _ in range(3)]
    if seq is not None:
        nt = seq // tm
        out_shape += [out_sds(BF16), jax.ShapeDtypeStruct((m // seq, SB_HEADS, nt, SB_HEAD_DIM, tm), BF16)]
        out_specs += [pl.BlockSpec((tm, width), row),
                      pl.BlockSpec((None, SB_HEADS, None, SB_HEAD_DIM, tm), lambda i: (i // nt, 0, i % nt, 0, 0))]
    return pl.pallas_call(
        functools.partial(_sb_proj_kernel, width=width),
        out_shape=tuple(out_shape),
        grid=(m // tm,),
        in_specs=[pl.BlockSpec((tm, d), row), pl.BlockSpec((d, 3 * width), lambda i: (0, 0))],
        out_specs=tuple(out_specs),
        compiler_params=_params("parallel"),
        name="sb_proj",
    )(x, w_sb)


def _mla_q_proj_kernel(x_ref, wlat_ref, gq_ref, gkv_ref, wuq_ref, cos_ref, sin_ref,
                       q_ref, ckv_ref, kr_ref, *, q_lora, kv_lora):
    xb = x_ref[...].astype(BF16)
    cos = cos_ref[...]
    sin = sin_ref[...]
    ckv = _dot(xb, wlat_ref[:, q_lora:q_lora + kv_lora])
    ckv_ref[...] = _rms(ckv, gkv_ref[...])
    kr2 = _dot(xb, wlat_ref[:, q_lora + kv_lora:])
    kr_ref[...] = kr2[:, :ROPE_DIM] * cos[:, :ROPE_DIM] + kr2[:, ROPE_DIM:] * sin[:, :ROPE_DIM]
    cq = _dot(xb, wlat_ref[:, 0:q_lora])
    cqn = _rms(cq, gq_ref[...]).astype(BF16)
    n_nope = MLA_HEADS * QK_NOPE_DIM
    n_rope = MLA_HEADS * ROPE_DIM
    for h in range(MLA_HEADS):
        qn = _dot(cqn, wuq_ref[:, h * QK_NOPE_DIM:(h + 1) * QK_NOPE_DIM])
        q_ref[h, :, 0:QK_NOPE_DIM] = (qn * MLA_SCALE).astype(BF16)
    for hp in range(MLA_HEADS // 2):
        lo = n_nope + hp * LANES
        qr = _dot(cqn, wuq_ref[:, lo:lo + LANES])
        qrr = _dot(cqn, wuq_ref[:, lo + n_rope:lo + n_rope + LANES])
        rot = ((qr * cos + qrr * sin) * MLA_SCALE).astype(BF16)
        q_ref[2 * hp, :, QK_NOPE_DIM:QK_DIM] = rot[:, :ROPE_DIM]
        q_ref[2 * hp + 1, :, QK_NOPE_DIM:QK_DIM] = rot[:, ROPE_DIM:]


def _mla_q_proj(x, w_lat, gq, gkv, w_uq_all, cos, sin, tm=ROW_TILE):
    m, d = x.shape
    q_lora, kv_lora = gq.shape[1], gkv.shape[1]
    n_pos_tiles = cos.shape[0] // tm
    row = lambda i: (i, 0)
    pos = lambda i: (i % n_pos_tiles, 0)
    const = lambda i: (0, 0)
    return pl.pallas_call(
        functools.partial(_mla_q_proj_kernel, q_lora=q_lora, kv_lora=kv_lora),
        out_shape=(jax.ShapeDtypeStruct((MLA_HEADS, m, QK_DIM), BF16),
                   jax.ShapeDtypeStruct((m, kv_lora), F32),
                   jax.ShapeDtypeStruct((m, ROPE_DIM), F32)),
        grid=(m // tm,),
        in_specs=[pl.BlockSpec((tm, d), row),
                  pl.BlockSpec(w_lat.shape, const),
                  pl.BlockSpec((1, q_lora), const),
                  pl.BlockSpec((1, kv_lora), const),
                  pl.BlockSpec(w_uq_all.shape, const),
                  pl.BlockSpec((tm, LANES), pos),
                  pl.BlockSpec((tm, LANES), pos)],
        out_specs=(pl.BlockSpec((MLA_HEADS, tm, QK_DIM), lambda i: (0, i, 0)),
                   pl.BlockSpec((tm, kv_lora), row),
                   pl.BlockSpec((tm, ROPE_DIM), row)),
        compiler_params=_params("parallel"),
        name="mla_q_proj",
    )(x, w_lat, gq, gkv, w_uq_all, cos, sin)


def _mla_kv_proj_kernel(ckv_ref, kr_ref, wukv_ref, kt_ref, v_ref):
    cb = ckv_ref[...].astype(BF16)
    kr = kr_ref[...]
    kr_t = jnp.concatenate([kr, kr], axis=1).T[0:ROPE_DIM, :].astype(BF16)
    n_k = MLA_HEADS * QK_NOPE_DIM
    for h in range(MLA_HEADS):
        kn = _dot(cb, wukv_ref[:, h * QK_NOPE_DIM:(h + 1) * QK_NOPE_DIM])
        kt_ref[h, 0:QK_NOPE_DIM, :] = kn.T.astype(BF16)
        kt_ref[h, QK_NOPE_DIM:QK_DIM, :] = kr_t
        vv = _dot(cb, wukv_ref[:, n_k + h * V_HEAD_DIM:n_k + (h + 1) * V_HEAD_DIM])
        v_ref[h] = vv.astype(BF16)


def _mla_kv_proj(ckv, kr, w_ukv_all, tm):
    m, kv_lora = ckv.shape
    row = lambda i: (i, 0)
    return pl.pallas_call(
        _mla_kv_proj_kernel,
        out_shape=(jax.ShapeDtypeStruct((MLA_HEADS, m // tm, QK_DIM, tm), BF16),
                   jax.ShapeDtypeStruct((MLA_HEADS, m, V_HEAD_DIM), BF16)),
        grid=(m // tm,),
        in_specs=[pl.BlockSpec((tm, kv_lora), row),
                  pl.BlockSpec((tm, ROPE_DIM), row),
                  pl.BlockSpec(w_ukv_all.shape, lambda i: (0, 0))],
        out_specs=(pl.BlockSpec((MLA_HEADS, None, QK_DIM, tm), lambda i: (0, i, 0, 0)),
                   pl.BlockSpec((MLA_HEADS, tm, V_HEAD_DIM), lambda i: (0, i, 0))),
        compiler_params=_params("parallel"),
        name="mla_kv_proj",
    )(ckv, kr, w_ukv_all)


def _gates_kernel(x_ref, w_ref, o_ref):
    z = _dot(x_ref[...].astype(BF16), w_ref[...])
    o_ref[...] = 1.0 / (1.0 + jnp.exp(-z))


def _gates(x, w_g, tm=512, tn=1024):
    m, d = x.shape
    n = w_g.shape[1]
    return pl.pallas_call(
        _gates_kernel,
        out_shape=jax.ShapeDtypeStruct((m, n), F32),
        grid=(m // tm, n // tn),
        in_specs=[pl.BlockSpec((tm, d), lambda i, j: (i, 0)),
                  pl.BlockSpec((d, tn), lambda i, j: (0, j))],
        out_specs=pl.BlockSpec((tm, tn), lambda i, j: (i, j)),
        compiler_params=_params("parallel", "arbitrary"),
        name="gates",
    )(x, w_g)


def _sb_attn_kernel(q_ref, k_ref, vt_ref, o_ref, *, tq, tk, q_off, n_heads):
    i = pl.program_id(2)
    t0 = q_off + i * tq
    hd = SB_HEAD_DIM
    heads = range(n_heads)
    qs = [q_ref[:, h * hd:(h + 1) * hd] for h in heads]
    rel = lax.broadcasted_iota(jnp.int32, (tk, tq), 0) - lax.broadcasted_iota(jnp.int32, (tk, tq), 1)
    r = lax.broadcasted_iota(jnp.int32, (tk, tk), 0)
    c = lax.broadcasted_iota(jnp.int32, (tk, tk), 1)
    upper = (c >= r).astype(BF16)

    def cond(carry):
        j, runs, _ = carry
        lowest = functools.reduce(jnp.minimum, runs)
        return jnp.logical_and(j >= 0, jnp.min(lowest) < SB_SKIP)

    def body(carry):
        j, runs, accs = carry
        k0 = pl.multiple_of(j * tk, tk)
        mask = rel < (t0 - k0)
        new_runs, new_accs = [], []
        for h in heads:
            z = _dot_nt(k_ref[pl.ds(k0, tk), h * hd:(h + 1) * hd], qs[h])
            sp = jnp.maximum(z, 0.0) + jnp.log(1.0 + jnp.exp(-jnp.abs(z)))
            sp = jnp.where(mask, sp, 0.0)
            hi = sp.astype(BF16)
            lo = (sp - hi.astype(F32)).astype(BF16)
            cs = _dot(upper, hi) + _dot(upper, lo) + runs[h]
            a = jnp.where(mask, jnp.exp(z - cs), 0.0)
            new_accs.append(accs[h] + _dot(vt_ref[h, j], a.astype(BF16)))
            new_runs.append(cs[0:1, :])
        return j - 1, tuple(new_runs), tuple(new_accs)

    j_start = (t0 + tq - 2) // tk
    init = (j_start, tuple(jnp.zeros((1, tq), F32) for _ in heads),
            tuple(jnp.zeros((hd, tq), F32) for _ in heads))
    _, _, accs = lax.while_loop(cond, body, init)
    for h in heads:
        o_ref[:, h * hd:(h + 1) * hd] = accs[h].T.astype(o_ref.dtype)


def _sb_attn(q, k, vt, *, n_batch, tq_total, q_off, tq, tk, n_heads=4):
    tk_total = k.shape[0] // n_batch
    nq = tq_total // tq
    hw = n_heads * SB_HEAD_DIM
    return pl.pallas_call(
        functools.partial(_sb_attn_kernel, tq=tq, tk=tk, q_off=q_off, n_heads=n_heads),
        out_shape=jax.ShapeDtypeStruct((n_batch * tq_total, SB_HEADS * SB_HEAD_DIM), BF16),
        grid=(n_batch, SB_HEADS // n_heads, nq),
        in_specs=[pl.BlockSpec((tq, hw), lambda b, g, i: (b * nq + i, g)),
                  pl.BlockSpec((tk_total, hw), lambda b, g, i: (b, g)),
                  pl.BlockSpec((None, n_heads, tk_total // tk, SB_HEAD_DIM, tk),
                               lambda b, g, i: (b, g, 0, 0, 0))],
        out_specs=pl.BlockSpec((tq, hw), lambda b, g, i: (b * nq + i, g)),
        compiler_params=_params("parallel", "parallel", "arbitrary"),
        name="sb_attn",
    )(q, k, vt)


def _sb_decode_kernel(q_ref, kn_ref, vn_ref, kc_ref, vc_ref, o_ref, *, dec, tk):
    hd = SB_HEAD_DIM
    heads = range(SB_HEADS)
    rows = SB_HEADS * dec
    past = kc_ref.shape[0]

    def softplus(z):
        return jnp.maximum(z, 0.0) + jnp.log(1.0 + jnp.exp(-jnp.abs(z)))

    def lower(n):
        return (lax.broadcasted_iota(jnp.int32, (n, n), 0) >= lax.broadcasted_iota(jnp.int32, (n, n), 1)).astype(BF16)

    def reverse_cumsum(sp, tri):
        hi = sp.astype(BF16)
        lo = (sp - hi.astype(F32)).astype(BF16)
        return _dot(hi, tri) + _dot(lo, tri)

    def head_cols(x, h):
        return x[:, h * hd:(h + 1) * hd]

    qs = [head_cols(q_ref[...], h) for h in heads]

    kn = kn_ref[...].astype(BF16)
    vn = vn_ref[...].astype(BF16)
    z = jnp.concatenate([_dot_nt(qs[h], head_cols(kn, h)) for h in heads], axis=0)
    qt = lax.rem(lax.broadcasted_iota(jnp.int32, (rows, dec), 0), dec)
    mask = lax.broadcasted_iota(jnp.int32, (rows, dec), 1) < qt
    cs = reverse_cumsum(jnp.where(mask, softplus(z), 0.0), lower(dec))
    a = jnp.where(mask, jnp.exp(z - cs), 0.0).astype(BF16)
    acc0 = jnp.concatenate([_dot(a[h * dec:(h + 1) * dec, :], head_cols(vn, h)) for h in heads], axis=0)
    tri = lower(tk)

    def cond(carry):
        j, run, _ = carry
        return jnp.logical_and(j >= 0, jnp.min(run) < SB_SKIP)

    def body(carry):
        j, run, acc = carry
        k0 = pl.multiple_of(j * tk, tk)
        kc = kc_ref[pl.ds(k0, tk), :].astype(BF16)
        vc = vc_ref[pl.ds(k0, tk), :].astype(BF16)
        z = jnp.concatenate([_dot_nt(qs[h], head_cols(kc, h)) for h in heads], axis=0)
        cs = reverse_cumsum(softplus(z), tri) + run
        a = jnp.exp(z - cs).astype(BF16)
        pv = jnp.concatenate([_dot(a[h * dec:(h + 1) * dec, :], head_cols(vc, h)) for h in heads], axis=0)
        return j - 1, cs[:, 0:1], acc + pv

    _, _, acc = lax.while_loop(cond, body, (past // tk - 1, cs[:, 0:1], acc0))
    for h in heads:
        o_ref[:, h * hd:(h + 1) * hd] = acc[h * dec:(h + 1) * dec, :].astype(o_ref.dtype)


def _sb_decode(q, k_new, v_new, cache_k, cache_v, *, dec, tk=256):
    nd, past, width = cache_k.shape
    assert past % tk == 0
    new = pl.BlockSpec((dec, width), lambda b: (b, 0))
    cache = pl.BlockSpec((None, past, width), lambda b: (b, 0, 0))
    return pl.pallas_call(
        functools.partial(_sb_decode_kernel, dec=dec, tk=tk),
        out_shape=jax.ShapeDtypeStruct((nd * dec, width), BF16),
        grid=(nd,),
        in_specs=[new, new, new, cache, cache],
        out_specs=new,
        compiler_params=_params("parallel"),
        name="sb_decode",
    )(q, k_new, v_new, cache_k, cache_v)


def _mla_attn_kernel(q_ref, kt_ref, v_ref, o_ref, s_scr, p_scr, acc_scr, m_scr, *, t):
    i = pl.program_id(2)
    ones = jnp.ones((t, V_HEAD_DIM), BF16)

    def scores(j):
        return _dot(q_ref[...], kt_ref[j])

    def weighted_values(p, j):
        v_ext = jnp.concatenate([v_ref[pl.ds(pl.multiple_of(j * t, t), t), :], ones], axis=1)
        return _dot(p, v_ext)

    def step(j, a, diagonal=False):
        if not diagonal:
            s_scr[1 - a] = scores(j + 1)
        pv = weighted_values(p_scr[1 - a], jnp.maximum(j - 1, 0))
        s = s_scr[a]
        if diagonal:
            qchunk = lax.broadcasted_iota(jnp.int32, (t, t), 0) // CHUNK
            kchunk = lax.broadcasted_iota(jnp.int32, (t, t), 1) // CHUNK
            s = jnp.where(kchunk <= qchunk, s, NEG_INF)
        m = m_scr[...]
        m_new = jnp.maximum(m, jnp.max(s, axis=1, keepdims=True))
        p = jnp.exp2(s - m_new).astype(BF16)
        acc = jnp.exp2(m - m_new) * (acc_scr[...] + pv)
        if diagonal:
            acc = acc + weighted_values(p, j)
            o_ref[...] = (acc[:, :V_HEAD_DIM] / acc[:, V_HEAD_DIM:]).astype(o_ref.dtype)
        else:
            m_scr[...] = m_new
            p_scr[a] = p
            acc_scr[...] = acc

    s_scr[0] = scores(0)
    p_scr[1] = jnp.zeros((t, t), BF16)
    acc_scr[...] = jnp.zeros_like(acc_scr)
    m_scr[...] = jnp.full(m_scr.shape, NEG_INF, F32)

    def pair(jj, _):
        step(2 * jj, 0)
        step(2 * jj + 1, 1)
        return 0

    lax.fori_loop(0, i // 2, pair, 0)
    odd = lax.rem(i, 2) == 1

    @pl.when(odd)
    def _():
        step(i - 1, 0)
        step(i, 1, diagonal=True)

    @pl.when(jnp.logical_not(odd))
    def _():
        step(i, 0, diagonal=True)


def _mla_attn(q_cat, kt, v, *, n_batch, seq):
    t = kt.shape[3]
    assert seq % t == 0 and t % CHUNK == 0
    nq = seq // t
    return pl.pallas_call(
        functools.partial(_mla_attn_kernel, t=t),
        out_shape=jax.ShapeDtypeStruct((n_batch * seq, MLA_HEADS * V_HEAD_DIM), BF16),
        grid=(n_batch, MLA_HEADS, nq),
        in_specs=[pl.BlockSpec((None, t, QK_DIM), lambda b, h, i: (h, b * nq + i, 0)),
                  pl.BlockSpec((None, nq, QK_DIM, t), lambda b, h, i: (h, b, 0, 0)),
                  pl.BlockSpec((None, seq, V_HEAD_DIM), lambda b, h, i: (h, b, 0))],
        out_specs=pl.BlockSpec((t, V_HEAD_DIM), lambda b, h, i: (b * nq + i, h)),
        scratch_shapes=[pltpu.VMEM((2, t, t), F32), pltpu.VMEM((2, t, t), BF16),
                        pltpu.VMEM((t, 2 * V_HEAD_DIM), F32), pltpu.VMEM((t, 1), F32)],
        compiler_params=_params("parallel", "parallel", "arbitrary"),
        name="mla_attn",
    )(q_cat, kt, v)


def _mla_decode_kernel(q_ref, wuk_ref, wuv_ref, cckv_ref, ckr_ref, nckv_ref, nkr_ref, o_ref, *, past, dec):
    rows = MLA_HEADS * dec
    ql = jnp.concatenate([_dot(q_ref[h, :, 0:QK_NOPE_DIM], wuk_ref[h]) for h in range(MLA_HEADS)], axis=0)
    ql = ql.astype(BF16)
    qr = jnp.concatenate([q_ref[h, :, QK_NOPE_DIM:QK_DIM] for h in range(MLA_HEADS)], axis=0)
    kc = cckv_ref[...].astype(BF16)
    kn = nckv_ref[...].astype(BF16)
    s_c = _dot_nt(ql, kc) + _dot_nt(qr, ckr_ref[...].astype(BF16))
    s_n = _dot_nt(ql, kn) + _dot_nt(qr, nkr_ref[...].astype(BF16))
    qpos = past + lax.rem(lax.broadcasted_iota(jnp.int32, (rows, dec), 0), dec)
    kpos = past + lax.broadcasted_iota(jnp.int32, (rows, dec), 1)
    s_n = jnp.where(kpos // CHUNK <= qpos // CHUNK, s_n, NEG_INF)
    m = jnp.maximum(jnp.max(s_c, axis=1, keepdims=True), jnp.max(s_n, axis=1, keepdims=True))
    p_c = jnp.exp2(s_c - m)
    p_n = jnp.exp2(s_n - m)
    l = jnp.sum(p_c, axis=1, keepdims=True) + jnp.sum(p_n, axis=1, keepdims=True)
    ol = (_dot(p_c.astype(BF16), kc) + _dot(p_n.astype(BF16), kn)) / l
    ol = ol.astype(BF16)
    for h in range(MLA_HEADS):
        o_h = _dot(ol[h * dec:(h + 1) * dec, :], wuv_ref[h])
        o_ref[:, h * V_HEAD_DIM:(h + 1) * V_HEAD_DIM] = o_h.astype(o_ref.dtype)


def _mla_decode(q_cat, w_uk, w_uv, cache_ckv, cache_kr, ckv, kr, *, q_blk0, dec):
    nd, past, kv_lora = cache_ckv.shape
    const3 = lambda b: (0, 0, 0)
    return pl.pallas_call(
        functools.partial(_mla_decode_kernel, past=past, dec=dec),
        out_shape=jax.ShapeDtypeStruct((nd * dec, MLA_HEADS * V_HEAD_DIM), BF16),
        grid=(nd,),
        in_specs=[pl.BlockSpec((MLA_HEADS, dec, QK_DIM), lambda b: (0, q_blk0 + b, 0)),
                  pl.BlockSpec(w_uk.shape, const3),
                  pl.BlockSpec(w_uv.shape, const3),
                  pl.BlockSpec((None, past, kv_lora), lambda b: (b, 0, 0)),
                  pl.BlockSpec((None, past, ROPE_DIM), lambda b: (b, 0, 0)),
                  pl.BlockSpec((dec, kv_lora), lambda b: (q_blk0 + b, 0)),
                  pl.BlockSpec((dec, ROPE_DIM), lambda b: (q_blk0 + b, 0))],
        out_specs=pl.BlockSpec((dec, MLA_HEADS * V_HEAD_DIM), lambda b: (b, 0)),
        compiler_params=_params("parallel"),
        name="mla_decode",
    )(q_cat, w_uk, w_uv, cache_ckv, cache_kr, ckv, kr)


def _merge_kernel(x_ref, osb_ref, omla_ref, ga_ref, gb_ref, wa_ref, wb_ref, wo_ref, g_ref, b_ref, *rest,
                  alpha, n_own):
    o_ref, acc_ref = rest[-2:]
    i = pl.program_id(0)
    j = pl.program_id(1)
    last = j == pl.num_programs(1) - 1

    @pl.when(j == 0)
    def _():
        acc_ref[...] = jnp.zeros_like(acc_ref)

    @pl.when(i < n_own)
    def _():
        u = ga_ref[...] * _dot(osb_ref[...], wa_ref[...]) + gb_ref[...] * _dot(omla_ref[...], wb_ref[...])
        acc_ref[...] += _dot(u.astype(BF16), wo_ref[...])

    @pl.when(jnp.logical_and(last, i < n_own))
    def _():
        o_ref[...] = _layer_norm(alpha * x_ref[...] + acc_ref[...], g_ref[...], b_ref[...])

    @pl.when(jnp.logical_and(last, i >= n_own))
    def _():
        o_ref[...] = jnp.zeros_like(o_ref)


def _merge(x, o_sb, o_mla, gates, w_br_a, w_br_b, w_o, ln_g, ln_b, *, alpha, out_rows, row0=0, into=None,
           tm=512, tn=512):
    m, d = x.shape
    nj = d // tn
    blk0 = row0 // tm
    n_own = m // tm
    n_rows_tiles = n_own if into is not None else out_rows // tm
    own = lambda i: jnp.minimum(i, n_own - 1)
    extra_specs = [] if into is None else [pl.BlockSpec(memory_space=pl.ANY)]
    extra_args = [] if into is None else [into]
    return pl.pallas_call(
        functools.partial(_merge_kernel, alpha=alpha, n_own=n_own),
        out_shape=jax.ShapeDtypeStruct((out_rows, d), F32),
        grid=(n_rows_tiles, nj),
        input_output_aliases={} if into is None else {10: 0},
        in_specs=[pl.BlockSpec((tm, d), lambda i, j: (own(i), 0)),
                  pl.BlockSpec((tm, o_sb.shape[1]), lambda i, j: (own(i), 0)),
                  pl.BlockSpec((tm, o_mla.shape[1]), lambda i, j: (own(i), 0)),
                  pl.BlockSpec((tm, tn), lambda i, j: (own(i), j)),
                  pl.BlockSpec((tm, tn), lambda i, j: (own(i), nj + j)),
                  pl.BlockSpec((w_br_a.shape[0], tn), lambda i, j: (0, j)),
                  pl.BlockSpec((w_br_b.shape[0], tn), lambda i, j: (0, j)),
                  pl.BlockSpec((tn, d), lambda i, j: (j, 0)),
                  pl.BlockSpec((1, d), lambda i, j: (0, 0)),
                  pl.BlockSpec((1, d), lambda i, j: (0, 0))] + extra_specs,
        out_specs=pl.BlockSpec((tm, d), lambda i, j: (blk0 + i, 0)),
        scratch_shapes=[pltpu.VMEM((tm, d), F32)],
        compiler_params=_params("parallel", "arbitrary"),
        name="merge_ln1",
    )(x, o_sb, o_mla, gates, gates, w_br_a, w_br_b, w_o, ln_g, ln_b, *extra_args)


def _router_kernel(x_ref, wh_ref, wl_ref, o_ref):
    x = x_ref[...]
    xh = x.astype(BF16)
    xl = (x - xh.astype(F32)).astype(BF16)
    logits = _dot(xh, wh_ref[...]) + (_dot(xh, wl_ref[...]) + _dot(xl, wh_ref[...]))
    lane = lax.broadcasted_iota(jnp.int32, logits.shape, 1)
    big = jnp.int32(LANES)
    is_grp = lane < N_GROUPS
    gl = jnp.where(is_grp, logits, -jnp.inf)
    gmax = jnp.max(gl, axis=1, keepdims=True)
    gsum = jnp.sum(jnp.where(is_grp, jnp.exp(gl - gmax), 0.0), axis=1, keepdims=True)
    p_grp = 1.0 / gsum
    grp = jnp.min(jnp.where(jnp.logical_and(is_grp, gl == gmax), lane, big), axis=1, keepdims=True)
    lo = N_GROUPS + grp * EXPERTS_PER_GROUP
    in_grp = jnp.logical_and(lane >= lo, lane < lo + EXPERTS_PER_GROUP)
    el = jnp.where(in_grp, logits, -jnp.inf)
    v1 = jnp.max(el, axis=1, keepdims=True)
    i1 = jnp.min(jnp.where(el == v1, lane, big), axis=1, keepdims=True)
    el2 = jnp.where(lane == i1, -jnp.inf, el)
    v2 = jnp.max(el2, axis=1, keepdims=True)
    i2 = jnp.min(jnp.where(el2 == v2, lane, big), axis=1, keepdims=True)
    e2w = jnp.exp(v2 - v1)
    w1 = p_grp / (1.0 + e2w)
    w2 = p_grp * e2w / (1.0 + e2w)
    out = jnp.where(lane == 0, (i1 - N_GROUPS).astype(F32),
                    jnp.where(lane == 1, (i2 - N_GROUPS).astype(F32),
                              jnp.where(lane == 2, w1, jnp.where(lane == 3, w2, 0.0))))
    o_ref[...] = out


def _router(x1, w_hi, w_lo, tm=ROW_TILE):
    m, d = x1.shape
    return pl.pallas_call(
        _router_kernel,
        out_shape=jax.ShapeDtypeStruct((m, LANES), F32),
        grid=(m // tm,),
        in_specs=[pl.BlockSpec((tm, d), lambda i: (i, 0)),
                  pl.BlockSpec((d, LANES), lambda i: (0, 0)),
                  pl.BlockSpec((d, LANES), lambda i: (0, 0))],
        out_specs=pl.BlockSpec((tm, LANES), lambda i: (i, 0)),
        compiler_params=_params("parallel"),
        name="router",
    )(x1, w_hi, w_lo)


def _row_gather_start(idx_ref, src_hbm, dst_ref, sem, n_rows, unrolled=False):
    def start(r):
        tok = idx_ref[0, 0, r]
        pltpu.make_async_copy(src_hbm.at[pl.ds(tok, 1)], dst_ref.at[pl.ds(r, 1)], sem).start()

    if unrolled:
        for r in range(n_rows):
            start(r)
    else:
        def body(r, _):
            start(r)
            return 0
        lax.fori_loop(0, n_rows, body, 0)


def _row_gather_wait(src_hbm, dst_ref, sem, n_rows):
    pltpu.make_async_copy(src_hbm.at[pl.ds(0, n_rows)], dst_ref, sem).wait()


def _moe_kernel(te_ref, nu_ref, idx_ref, idx_next_ref, x_hbm, wg_ref, wu_ref, wd_ref, rw_ref,
                y_ref, xbuf0, xbuf1, sem, wgb, wub, wdb, *, tm):
    t = pl.program_id(0)
    n_used = nu_ref[0]
    bufs = (xbuf0, xbuf1)

    @pl.when(t == 0)
    def _():
        _row_gather_start(idx_ref, x_hbm, xbuf0, sem.at[0], tm)

    @pl.when(jnp.logical_or(t == 0, te_ref[t] != te_ref[jnp.maximum(t - 1, 0)]))
    def _():
        wgb[...] = wg_ref[...].astype(BF16)
        wub[...] = wu_ref[...].astype(BF16)
        wdb[...] = wd_ref[...].astype(BF16)

    def run(a):
        @pl.when(t <= n_used)
        def _():
            _row_gather_wait(x_hbm, bufs[a], sem.at[a], tm)

        @pl.when(t < n_used)
        def _():
            _row_gather_start(idx_next_ref, x_hbm, bufs[1 - a], sem.at[1 - a], tm, unrolled=True)
            xb = bufs[a][...].astype(BF16)
            g = _dot(xb, wgb[...])
            u = _dot(xb, wub[...])
            h = (g / (1.0 + jnp.exp(-g))) * u
            y_ref[...] = _dot(h.astype(BF16), wdb[...]) * rw_ref[...]

    parity = lax.rem(t, 2)
    pl.when(parity == 0)(lambda: run(0))
    pl.when(parity == 1)(lambda: run(1))

    @pl.when(t >= n_used)
    def _():
        y_ref[...] = jnp.zeros_like(y_ref)


def _moe(tile_expert, n_used, row_token, x1, w_gate, w_up, w_down, row_w, *, tm):
    n_tiles = tile_expert.shape[0]
    d = x1.shape[1]
    f = w_gate.shape[2]
    idx3 = row_token.reshape(n_tiles, 1, tm)
    wmap = lambda t, te, nu: (te[t], 0, 0)
    grid_spec = pltpu.PrefetchScalarGridSpec(
        num_scalar_prefetch=2,
        grid=(n_tiles,),
        in_specs=[pl.BlockSpec((1, 1, tm), lambda t, te, nu: (t, 0, 0), memory_space=pltpu.SMEM),
                  pl.BlockSpec((1, 1, tm), lambda t, te, nu: (jnp.minimum(t + 1, n_tiles - 1), 0, 0),
                               memory_space=pltpu.SMEM),
                  pl.BlockSpec(memory_space=pl.ANY),
                  pl.BlockSpec((None, d, f), wmap),
                  pl.BlockSpec((None, d, f), wmap),
                  pl.BlockSpec((None, f, d), wmap),
                  pl.BlockSpec((tm, 1), lambda t, te, nu: (t, 0))],
        out_specs=pl.BlockSpec((tm, d), lambda t, te, nu: (t, 0)),
        scratch_shapes=[pltpu.VMEM((tm, d), F32),
                        pltpu.VMEM((tm, d), F32),
                        pltpu.SemaphoreType.DMA((2,)),
                        pltpu.VMEM((d, f), BF16),
                        pltpu.VMEM((d, f), BF16),
                        pltpu.VMEM((f, d), BF16)],
    )
    return pl.pallas_call(
        functools.partial(_moe_kernel, tm=tm),
        out_shape=jax.ShapeDtypeStruct((n_tiles * tm, d), F32),
        grid_spec=grid_spec,
        compiler_params=_params("arbitrary"),
        name="moe_experts",
    )(tile_expert, n_used, idx3, idx3, x1, w_gate, w_up, w_down, row_w)


def _final_kernel(p1_ref, p2_ref, p1n_ref, p2n_ref, y_hbm, x1_ref, pe_ref, wpp_ref, wpg_ref, g_ref, b_ref,
                  o_ref, ya0, yb0, ya1, yb1, sem, *, tm, alpha):
    t = pl.program_id(0)
    last = pl.num_programs(0) - 1
    bufs = ((ya0, yb0), (ya1, yb1))

    @pl.when(t == 0)
    def _():
        _row_gather_start(p1_ref, y_hbm, ya0, sem.at[0], tm)
        _row_gather_start(p2_ref, y_hbm, yb0, sem.at[0], tm)

    def run(a):
        ya, yb = bufs[a]
        na, nb = bufs[1 - a]
        _row_gather_wait(y_hbm, ya, sem.at[a], tm)
        _row_gather_wait(y_hbm, yb, sem.at[a], tm)
        _row_gather_start(p1n_ref, y_hbm, na, sem.at[1 - a], tm, unrolled=True)
        _row_gather_start(p2n_ref, y_hbm, nb, sem.at[1 - a], tm, unrolled=True)
        f = ya[...] + yb[...]
        x2 = _layer_norm(alpha * x1_ref[...] + f, g_ref[...], b_ref[...])
        gate = 1.0 / (1.0 + jnp.exp(-_dot(x2.astype(BF16), wpg_ref[...])))
        o_ref[...] = x2 + gate * _dot(pe_ref[...].astype(BF16), wpp_ref[...])

        @pl.when(t == last)
        def _():
            _row_gather_wait(y_hbm, na, sem.at[1 - a], tm)
            _row_gather_wait(y_hbm, nb, sem.at[1 - a], tm)

    parity = lax.rem(t, 2)
    pl.when(parity == 0)(lambda: run(0))
    pl.when(parity == 1)(lambda: run(1))


def _final(pos1, pos2, y_sorted, x1, x1_blk0, pe, w_pp, w_pg, ln_g, ln_b, *, alpha, tm=ROW_TILE):
    d = x1.shape[1]
    m = pe.shape[0]
    n_tiles = m // tm
    p1 = pos1.reshape(n_tiles, 1, tm)
    p2 = pos2.reshape(n_tiles, 1, tm)
    cur = lambda t: (t, 0, 0)
    nxt = lambda t: (jnp.minimum(t + 1, n_tiles - 1), 0, 0)
    smem = lambda im: pl.BlockSpec((1, 1, tm), im, memory_space=pltpu.SMEM)
    const = lambda t: (0, 0)
    return pl.pallas_call(
        functools.partial(_final_kernel, tm=tm, alpha=alpha),
        out_shape=jax.ShapeDtypeStruct((m, d), F32),
        grid=(n_tiles,),
        in_specs=[smem(cur), smem(cur), smem(nxt), smem(nxt),
                  pl.BlockSpec(memory_space=pl.ANY),
                  pl.BlockSpec((tm, d), lambda t: (x1_blk0 + t, 0)),
                  pl.BlockSpec((tm, pe.shape[1]), lambda t: (t, 0)),
                  pl.BlockSpec(w_pp.shape, const),
                  pl.BlockSpec(w_pg.shape, const),
                  pl.BlockSpec((1, d), const),
                  pl.BlockSpec((1, d), const)],
        out_specs=pl.BlockSpec((tm, d), lambda t: (t, 0)),
        scratch_shapes=[pltpu.VMEM((tm, d), F32) for _ in range(4)] + [pltpu.SemaphoreType.DMA((2,))],
        compiler_params=_params("arbitrary"),
        name="combine_ln2_ple",
    )(p1, p2, p1, p2, y_sorted, x1, pe, w_pp, w_pg, ln_g, ln_b)


def _rotate_half_cols(w):
    half = ROPE_DIM // 2
    return jnp.concatenate([-w[..., half:], w[..., :half]], axis=-1)


def _rope_tables(pos):
    inv_freq = 1.0 / (ROPE_THETA ** (jnp.arange(0, ROPE_DIM, 2, dtype=F32) / ROPE_DIM))
    ang = pos.astype(F32)[:, None] * inv_freq[None, :]
    ang = jnp.concatenate([ang, ang, ang, ang], axis=-1)
    return jnp.cos(ang), jnp.sin(ang)


def _route_tables(route, tm):
    m = route.shape[0]
    e_pair = jnp.concatenate([route[:, 0], route[:, 1]]).astype(jnp.int32)
    w_pair = jnp.concatenate([route[:, 2], route[:, 3]])
    tok = jnp.arange(m, dtype=jnp.int32)
    tok_pair = jnp.concatenate([tok, tok])
    onehot = (e_pair[:, None] == jnp.arange(N_EXPERTS, dtype=jnp.int32)[None, :]).astype(jnp.int32)
    csum = jnp.cumsum(onehot, axis=0)
    rank = jnp.sum(csum * onehot, axis=1) - 1
    counts = csum[-1]
    padded = ((counts + tm - 1) // tm) * tm
    pad_end = jnp.cumsum(padded)
    pad_off = pad_end - padded
    pos = pad_off[e_pair] + rank
    n_rows = (2 * m // tm + N_EXPERTS) * tm
    payload = jnp.stack([tok_pair, lax.bitcast_convert_type(w_pair, jnp.int32)], axis=1)
    rows = jnp.zeros((n_rows, 2), jnp.int32).at[pos].set(payload)
    row_token = rows[:, 0]
    row_w = lax.bitcast_convert_type(rows[:, 1], F32)
    tile_start = jnp.arange(n_rows // tm, dtype=jnp.int32) * tm
    tile_expert = jnp.sum((pad_end[None, :] <= tile_start[:, None]).astype(jnp.int32), axis=1)
    tile_expert = jnp.minimum(tile_expert, N_EXPERTS - 1)
    n_used = (pad_end[-1] // tm).astype(jnp.int32).reshape(1)
    last_expert = tile_expert[jnp.maximum(n_used[0] - 1, 0)]
    tile_expert = jnp.where(tile_start // tm < n_used[0], tile_expert, last_expert).astype(jnp.int32)
    return tile_expert, n_used, row_token, row_w.reshape(n_rows, 1), pos[:m], pos[m:]


def kernel(x_prompt, x_sample, cache_sb_k, cache_sb_v, cache_mla_ckv, cache_mla_krope, p_prompt, p_sample,
           w_in, q_a_norm, kv_a_norm, w_uq, w_ukv, w_br_a, w_br_b, w_o, ln1_g, ln1_b, ln2_g, ln2_b,
           w_router_group, w_router_expert, w_exp_gate, w_exp_up, w_exp_down, w_ple_proj, w_ple_gate):
    depth = w_in.shape[0]
    assert depth == 1, "single trunk layer"
    nb, seq, d = x_prompt.shape
    nd, dec, _ = x_sample.shape
    past = cache_sb_k.shape[2]
    sbw = SB_HEADS * SB_HEAD_DIM
    q_lora = q_a_norm.shape[1]
    kv_lora = kv_a_norm.shape[1]
    alpha = (2.0 * depth) ** 0.25
    mp = nb * seq
    ms = nd * dec

    w0 = w_in[0]
    o1 = 3 * sbw
    o2 = o1 + q_lora
    o3 = o2 + kv_lora
    o4 = o3 + ROPE_DIM
    w_sb = w0[:, :o1].astype(BF16)
    w_kr = w0[:, o3:o4]
    w_lat = jnp.concatenate([w0[:, o1:o3], w_kr, _rotate_half_cols(w_kr)], axis=1).astype(BF16)
    w_g = w0[:, o4:].astype(BF16)
    wq = w_uq[0].reshape(q_lora, MLA_HEADS, QK_DIM)
    wq_r = wq[:, :, QK_NOPE_DIM:]
    w_uq_all = jnp.concatenate([wq[:, :, :QK_NOPE_DIM].reshape(q_lora, -1), wq_r.reshape(q_lora, -1),
                                _rotate_half_cols(wq_r).reshape(q_lora, -1)], axis=1).astype(BF16)
    wkv = w_ukv[0].reshape(kv_lora, MLA_HEADS, QK_NOPE_DIM + V_HEAD_DIM)
    w_ukv_all = jnp.concatenate([wkv[:, :, :QK_NOPE_DIM].reshape(kv_lora, -1),
                                 wkv[:, :, QK_NOPE_DIM:].reshape(kv_lora, -1)], axis=1).astype(BF16)
    w_r = jnp.concatenate([w_router_group[0], w_router_expert[0]], axis=1)
    w_r = jnp.pad(w_r, ((0, 0), (0, LANES - w_r.shape[1])))
    w_r_hi = w_r.astype(BF16)
    w_r_lo = (w_r - w_r_hi.astype(F32)).astype(BF16)

    w_uk = jnp.transpose(wkv[:, :, :QK_NOPE_DIM], (1, 2, 0)).astype(BF16)
    w_uv = jnp.transpose(wkv[:, :, QK_NOPE_DIM:], (1, 0, 2)).astype(BF16)
    w_a, w_b, w_out = w_br_a[0].astype(BF16), w_br_b[0].astype(BF16), w_o[0].astype(BF16)
    w_pp, w_pg = w_ple_proj[0].astype(BF16), w_ple_gate[0].astype(BF16)

    xp = x_prompt.reshape(mp, d)
    cos_p, sin_p = _rope_tables(jnp.arange(seq, dtype=jnp.int32))
    tile = 256
    sbq_p, sbk_p, sbv_p, sbk_b, vt_p = _sb_proj(xp, w_sb, seq=seq, tm=tile)
    q_cat_p, ckv_p, kr_p = _mla_q_proj(xp, w_lat, q_a_norm, kv_a_norm, w_uq_all, cos_p, sin_p)
    gates_p = _gates(xp, w_g)
    o_sb_p = _sb_attn(sbq_p, sbk_b, vt_p, n_batch=nb, tq_total=seq, q_off=0, tq=tile, tk=tile)
    kt_mla, v_mla = _mla_kv_proj(ckv_p, kr_p, w_ukv_all, tm=512)
    o_mla_p = _mla_attn(q_cat_p, kt_mla, v_mla, n_batch=nb, seq=seq)

    xs = x_sample.reshape(ms, d)
    cos_s, sin_s = _rope_tables(jnp.tile(past + jnp.arange(dec, dtype=jnp.int32), ROW_TILE // dec))
    sbq_s, sbk_s, sbv_s = _sb_proj(xs, w_sb)
    q_cat_s, ckv_s, kr_s = _mla_q_proj(xs, w_lat, q_a_norm, kv_a_norm, w_uq_all, cos_s, sin_s)
    gates_s = _gates(xs, w_g)
    o_sb_s = _sb_decode(sbq_s, sbk_s, sbv_s, cache_sb_k[0].reshape(nd, past, sbw),
                        cache_sb_v[0].reshape(nd, past, sbw), dec=dec)
    o_mla_s = _mla_decode(q_cat_s, w_uk, w_uv, cache_mla_ckv[0], cache_mla_krope[0], ckv_s, kr_s,
                          q_blk0=0, dec=dec)

    x1 = _merge(xp, o_sb_p, o_mla_p, gates_p, w_a, w_b, w_out, ln1_g, ln1_b, alpha=alpha, out_rows=mp + ms)
    x1 = _merge(xs, o_sb_s, o_mla_s, gates_s, w_a, w_b, w_out, ln1_g, ln1_b, alpha=alpha, out_rows=mp + ms,
                row0=mp, into=x1)
    route = _router(x1, w_r_hi, w_r_lo)
    tile_expert, n_used, row_token, row_w, pos1, pos2 = _route_tables(route, ROW_TILE)
    y_sorted = _moe(tile_expert, n_used, row_token, x1, w_exp_gate[0], w_exp_up[0], w_exp_down[0], row_w,
                    tm=ROW_TILE)
    y_p = _final(pos1[:mp], pos2[:mp], y_sorted, x1, 0, p_prompt[0].reshape(mp, -1), w_pp, w_pg,
                 ln2_g, ln2_b, alpha=alpha)
    y_s = _final(pos1[mp:], pos2[mp:], y_sorted, x1, mp // ROW_TILE, p_sample[0].reshape(ms, -1), w_pp, w_pg,
                 ln2_g, ln2_b, alpha=alpha)

    hs = (SB_HEADS, SB_HEAD_DIM)
    return (y_p.reshape(nb, seq, d), y_s.reshape(nd, dec, d),
            sbk_p.reshape((1, nb, seq) + hs), sbv_p.reshape((1, nb, seq) + hs),
            ckv_p.reshape(1, nb, seq, kv_lora), kr_p.reshape(1, nb, seq, ROPE_DIM),
            sbk_s.reshape((1, nd, dec) + hs), sbv_s.reshape((1, nd, dec) + hs),
            ckv_s.reshape(1, nd, dec, kv_lora), kr_s.reshape(1, nd, dec, ROPE_DIM))
```

```python
import functools

import jax
import jax.numpy as jnp
from jax import lax
from jax.experimental import pallas as pl
from jax.experimental.pallas import tpu as pltpu

F32 = jnp.float32
BF16 = jnp.bfloat16

SB_HEADS = 8
SB_HEAD_DIM = 128
MLA_HEADS = 16
QK_NOPE_DIM = 128
ROPE_DIM = 64
V_HEAD_DIM = 128
QK_DIM = QK_NOPE_DIM + ROPE_DIM
CHUNK = 64
N_GROUPS = 4
EXPERTS_PER_GROUP = 8
N_EXPERTS = N_GROUPS * EXPERTS_PER_GROUP
ROPE_THETA = 10000.0
LN_EPS = 1e-5
RMS_EPS = 1e-6
NEG_INF = -1e30
MLA_SCALE = QK_DIM ** -0.5 * 1.4426950408889634
SB_SCALE = SB_HEAD_DIM ** -0.5
SB_SKIP = 120.0

VMEM_LIMIT = 56 * 1024 * 1024
ROW_TILE = 256
LANES = 128


def _params(*sem):
    return pltpu.CompilerParams(dimension_semantics=sem, vmem_limit_bytes=VMEM_LIMIT)


def _dot(a, b):
    return jnp.dot(a, b, preferred_element_type=F32)


def _dot_nt(a, b):
    return lax.dot_general(a, b, (((1,), (1,)), ((), ())), preferred_element_type=F32)


def _rms(x, g):
    ms = jnp.mean(x * x, axis=-1, keepdims=True)
    return x * lax.rsqrt(ms + RMS_EPS) * g


def _layer_norm(x, g, b):
    mu = jnp.mean(x, axis=-1, keepdims=True)
    xc = x - mu
    var = jnp.mean(xc * xc, axis=-1, keepdims=True)
    return xc * lax.rsqrt(var + LN_EPS) * g + b


def _sb_proj_kernel(x_ref, w_ref, q_ref, k_ref, v_ref, *tile_refs, width):
    xb = x_ref[...].astype(BF16)
    q = _dot(xb, w_ref[:, 0:width])
    q_ref[...] = (q * SB_SCALE).astype(BF16)
    k = _dot(xb, w_ref[:, width:2 * width])
    k_ref[...] = k
    v = _dot(xb, w_ref[:, 2 * width:3 * width])
    v_ref[...] = v
    if tile_refs:
        kb_ref, vt_ref = tile_refs
        kb_ref[...] = k.astype(BF16)
        for h in range(SB_HEADS):
            vt_ref[h] = v[:, h * SB_HEAD_DIM:(h + 1) * SB_HEAD_DIM].T.astype(BF16)


def _sb_proj(x, w_sb, *, seq=None, tm=ROW_TILE):
    m, d = x.shape
    width = w_sb.shape[1] // 3
    row = lambda i: (i, 0)
    out_sds = lambda dt: jax.ShapeDtypeStruct((m, width), dt)
    out_shape = [out_sds(BF16), out_sds(F32), out_sds(F32)]
    out_specs = [pl.BlockSpec((tm, width), row) for _ in range(3)]
    if seq is not None:
        nt = seq // tm
        out_shape += [out_sds(BF16), jax.ShapeDtypeStruct((m // seq, SB_HEADS, nt, SB_HEAD_DIM, tm), BF16)]
        out_specs += [pl.BlockSpec((tm, width), row),
                      pl.BlockSpec((None, SB_HEADS, None, SB_HEAD_DIM, tm), lambda i: (i // nt, 0, i % nt, 0, 0))]
    return pl.pallas_call(
        functools.partial(_sb_proj_kernel, width=width),
        out_shape=tuple(out_shape),
        grid=(m // tm,),
        in_specs=[pl.BlockSpec((tm, d), row), pl.BlockSpec((d, 3 * width), lambda i: (0, 0))],
        out_specs=tuple(out_specs),
        compiler_params=_params("parallel"),
        name="sb_proj",
    )(x, w_sb)


def _mla_q_proj_kernel(x_ref, wlat_ref, gq_ref, gkv_ref, wuq_ref, cos_ref, sin_ref,
                       q_ref, ckv_ref, kr_ref, *, q_lora, kv_lora):
    xb = x_ref[...].astype(BF16)
    cos = cos_ref[...]
    sin = sin_ref[...]
    ckv = _dot(xb, wlat_ref[:, q_lora:q_lora + kv_lora])
    ckv_ref[...] = _rms(ckv, gkv_ref[...])
    kr2 = _dot(xb, wlat_ref[:, q_lora + kv_lora:])
    kr_ref[...] = kr2[:, :ROPE_DIM] * cos[:, :ROPE_DIM] + kr2[:, ROPE_DIM:] * sin[:, :ROPE_DIM]
    cq = _dot(xb, wlat_ref[:, 0:q_lora])
    cqn = _rms(cq, gq_ref[...]).astype(BF16)
    n_nope = MLA_HEADS * QK_NOPE_DIM
    n_rope = MLA_HEADS * ROPE_DIM
    for h in range(MLA_HEADS):
        qn = _dot(cqn, wuq_ref[:, h * QK_NOPE_DIM:(h + 1) * QK_NOPE_DIM])
        q_ref[h, :, 0:QK_NOPE_DIM] = (qn * MLA_SCALE).astype(BF16)
    for hp in range(MLA_HEADS // 2):
        lo = n_nope + hp * LANES
        qr = _dot(cqn, wuq_ref[:, lo:lo + LANES])
        qrr = _dot(cqn, wuq_ref[:, lo + n_rope:lo + n_rope + LANES])
        rot = ((qr * cos + qrr * sin) * MLA_SCALE).astype(BF16)
        q_ref[2 * hp, :, QK_NOPE_DIM:QK_DIM] = rot[:, :ROPE_DIM]
        q_ref[2 * hp + 1, :, QK_NOPE_DIM:QK_DIM] = rot[:, ROPE_DIM:]


def _mla_q_proj(x, w_lat, gq, gkv, w_uq_all, cos, sin, tm=ROW_TILE):
    m, d = x.shape
    q_lora, kv_lora = gq.shape[1], gkv.shape[1]
    n_pos_tiles = cos.shape[0] // tm
    row = lambda i: (i, 0)
    pos = lambda i: (i % n_pos_tiles, 0)
    const = lambda i: (0, 0)
    return pl.pallas_call(
        functools.partial(_mla_q_proj_kernel, q_lora=q_lora, kv_lora=kv_lora),
        out_shape=(jax.ShapeDtypeStruct((MLA_HEADS, m, QK_DIM), BF16),
                   jax.ShapeDtypeStruct((m, kv_lora), F32),
                   jax.ShapeDtypeStruct((m, ROPE_DIM), F32)),
        grid=(m // tm,),
        in_specs=[pl.BlockSpec((tm, d), row),
                  pl.BlockSpec(w_lat.shape, const),
                  pl.BlockSpec((1, q_lora), const),
                  pl.BlockSpec((1, kv_lora), const),
                  pl.BlockSpec(w_uq_all.shape, const),
                  pl.BlockSpec((tm, LANES), pos),
                  pl.BlockSpec((tm, LANES), pos)],
        out_specs=(pl.BlockSpec((MLA_HEADS, tm, QK_DIM), lambda i: (0, i, 0)),
                   pl.BlockSpec((tm, kv_lora), row),
                   pl.BlockSpec((tm, ROPE_DIM), row)),
        compiler_params=_params("parallel"),
        name="mla_q_proj",
    )(x, w_lat, gq, gkv, w_uq_all, cos, sin)


def _mla_kv_proj_kernel(ckv_ref, kr_ref, wukv_ref, kt_ref, v_ref):
    cb = ckv_ref[...].astype(BF16)
    kr = kr_ref[...]
    kr_t = jnp.concatenate([kr, kr], axis=1).T[0:ROPE_DIM, :].astype(BF16)
    n_k = MLA_HEADS * QK_NOPE_DIM
    for h in range(MLA_HEADS):
        kn = _dot(cb, wukv_ref[:, h * QK_NOPE_DIM:(h + 1) * QK_NOPE_DIM])
        kt_ref[h, 0:QK_NOPE_DIM, :] = kn.T.astype(BF16)
        kt_ref[h, QK_NOPE_DIM:QK_DIM, :] = kr_t
        vv = _dot(cb, wukv_ref[:, n_k + h * V_HEAD_DIM:n_k + (h + 1) * V_HEAD_DIM])
        v_ref[h] = vv.astype(BF16)


def _mla_kv_proj(ckv, kr, w_ukv_all, tm):
    m, kv_lora = ckv.shape
    row = lambda i: (i, 0)
    return pl.pallas_call(
        _mla_kv_proj_kernel,
        out_shape=(jax.ShapeDtypeStruct((MLA_HEADS, m // tm, QK_DIM, tm), BF16),
                   jax.ShapeDtypeStruct((MLA_HEADS, m, V_HEAD_DIM), BF16)),
        grid=(m // tm,),
        in_specs=[pl.BlockSpec((tm, kv_lora), row),
                  pl.BlockSpec((tm, ROPE_DIM), row),
                  pl.BlockSpec(w_ukv_all.shape, lambda i: (0, 0))],
        out_specs=(pl.BlockSpec((MLA_HEADS, None, QK_DIM, tm), lambda i: (0, i, 0, 0)),
                   pl.BlockSpec((MLA_HEADS, tm, V_HEAD_DIM), lambda i: (0, i, 0))),
        compiler_params=_params("parallel"),
        name="mla_kv_proj",
    )(ckv, kr, w_ukv_all)


def _gates_kernel(x_ref, w_ref, o_ref):
    z = _dot(x_ref[...].astype(BF16), w_ref[...])
    o_ref[...] = 1.0 / (1.0 + jnp.exp(-z))


def _gates(x, w_g, tm=512, tn=1024):
    m, d = x.shape
    n = w_g.shape[1]
    return pl.pallas_call(
        _gates_kernel,
        out_shape=jax.ShapeDtypeStruct((m, n), F32),
        grid=(m // tm, n // tn),
        in_specs=[pl.BlockSpec((tm, d), lambda i, j: (i, 0)),
                  pl.BlockSpec((d, tn), lambda i, j: (0, j))],
        out_specs=pl.BlockSpec((tm, tn), lambda i, j: (i, j)),
        compiler_params=_params("parallel", "arbitrary"),
        name="gates",
    )(x, w_g)


def _sb_attn_kernel(q_ref, k_ref, vt_ref, o_ref, *, tq, tk, q_off, n_heads):
    i = pl.program_id(2)
    t0 = q_off + i * tq
    hd = SB_HEAD_DIM
    heads = range(n_heads)
    qs = [q_ref[:, h * hd:(h + 1) * hd] for h in heads]
    rel = lax.broadcasted_iota(jnp.int32, (tk, tq), 0) - lax.broadcasted_iota(jnp.int32, (tk, tq), 1)
    r = lax.broadcasted_iota(jnp.int32, (tk, tk), 0)
    c = lax.broadcasted_iota(jnp.int32, (tk, tk), 1)
    upper = (c >= r).astype(BF16)

    def cond(carry):
        j, runs, _ = carry
        lowest = functools.reduce(jnp.minimum, runs)
        return jnp.logical_and(j >= 0, jnp.min(lowest) < SB_SKIP)

    def body(carry):
        j, runs, accs = carry
        k0 = pl.multiple_of(j * tk, tk)
        mask = rel < (t0 - k0)
        new_runs, new_accs = [], []
        for h in heads:
            z = _dot_nt(k_ref[pl.ds(k0, tk), h * hd:(h + 1) * hd], qs[h])
            sp = jnp.maximum(z, 0.0) + jnp.log(1.0 + jnp.exp(-jnp.abs(z)))
            sp = jnp.where(mask, sp, 0.0)
            hi = sp.astype(BF16)
            lo = (sp - hi.astype(F32)).astype(BF16)
            cs = _dot(upper, hi) + _dot(upper, lo) + runs[h]
            a = jnp.where(mask, jnp.exp(z - cs), 0.0)
            new_accs.append(accs[h] + _dot(vt_ref[h, j], a.astype(BF16)))
            new_runs.append(cs[0:1, :])
        return j - 1, tuple(new_runs), tuple(new_accs)

    j_start = (t0 + tq - 2) // tk
    init = (j_start, tuple(jnp.zeros((1, tq), F32) for _ in heads),
            tuple(jnp.zeros((hd, tq), F32) for _ in heads))
    _, _, accs = lax.while_loop(cond, body, init)
    for h in heads:
        o_ref[:, h * hd:(h + 1) * hd] = accs[h].T.astype(o_ref.dtype)


def _sb_attn(q, k, vt, *, n_batch, tq_total, q_off, tq, tk, n_heads=4):
    tk_total = k.shape[0] // n_batch
    nq = tq_total // tq
    hw = n_heads * SB_HEAD_DIM
    return pl.pallas_call(
        functools.partial(_sb_attn_kernel, tq=tq, tk=tk, q_off=q_off, n_heads=n_heads),
        out_shape=jax.ShapeDtypeStruct((n_batch * tq_total, SB_HEADS * SB_HEAD_DIM), BF16),
        grid=(n_batch, SB_HEADS // n_heads, nq),
        in_specs=[pl.BlockSpec((tq, hw), lambda b, g, i: (b * nq + i, g)),
                  pl.BlockSpec((tk_total, hw), lambda b, g, i: (b, g)),
                  pl.BlockSpec((None, n_heads, tk_total // tk, SB_HEAD_DIM, tk),
                               lambda b, g, i: (b, g, 0, 0, 0))],
        out_specs=pl.BlockSpec((tq, hw), lambda b, g, i: (b * nq + i, g)),
        compiler_params=_params("parallel", "parallel", "arbitrary"),
        name="sb_attn",
    )(q, k, vt)


def _sb_decode_kernel(q_ref, kn_ref, vn_ref, kc_ref, vc_ref, o_ref, *, dec, tk):
    hd = SB_HEAD_DIM
    heads = range(SB_HEADS)
    rows = SB_HEADS * dec
    past = kc_ref.shape[0] // SB_HEADS

    def softplus(z):
        return jnp.maximum(z, 0.0) + jnp.log(1.0 + jnp.exp(-jnp.abs(z)))

    def lower(n):
        return (lax.broadcasted_iota(jnp.int32, (n, n), 0) >= lax.broadcasted_iota(jnp.int32, (n, n), 1)).astype(BF16)

    def reverse_cumsum(sp, tri):
        hi = sp.astype(BF16)
        lo = (sp - hi.astype(F32)).astype(BF16)
        return _dot(hi, tri) + _dot(lo, tri)

    def head_cols(x, h):
        return x[:, h * hd:(h + 1) * hd]

    qs = [head_cols(q_ref[...], h) for h in heads]

    kn = kn_ref[...].astype(BF16)
    vn = vn_ref[...].astype(BF16)
    z = jnp.concatenate([_dot_nt(qs[h], head_cols(kn, h)) for h in heads], axis=0)
    qt = lax.rem(lax.broadcasted_iota(jnp.int32, (rows, dec), 0), dec)
    mask = lax.broadcasted_iota(jnp.int32, (rows, dec), 1) < qt
    cs = reverse_cumsum(jnp.where(mask, softplus(z), 0.0), lower(dec))
    a = jnp.where(mask, jnp.exp(z - cs), 0.0).astype(BF16)
    acc0 = jnp.concatenate([_dot(a[h * dec:(h + 1) * dec, :], head_cols(vn, h)) for h in heads], axis=0)
    tri = lower(tk)

    def cond(carry):
        j, run, _ = carry
        return jnp.logical_and(j >= 0, jnp.min(run) < SB_SKIP)

    def body(carry):
        j, run, acc = carry
        r0 = pl.multiple_of(j * (tk * SB_HEADS), tk * SB_HEADS)

        def head_rows(ref, h):
            return ref[pl.ds(r0 + h, tk, stride=SB_HEADS), :].astype(BF16)

        z = jnp.concatenate([_dot_nt(qs[h], head_rows(kc_ref, h)) for h in heads], axis=0)
        cs = reverse_cumsum(softplus(z), tri) + run
        a = jnp.exp(z - cs).astype(BF16)
        pv = jnp.concatenate([_dot(a[h * dec:(h + 1) * dec, :], head_rows(vc_ref, h)) for h in heads], axis=0)
        return j - 1, cs[:, 0:1], acc + pv

    _, _, acc = lax.while_loop(cond, body, (past // tk - 1, cs[:, 0:1], acc0))
    for h in heads:
        o_ref[:, h * hd:(h + 1) * hd] = acc[h * dec:(h + 1) * dec, :].astype(o_ref.dtype)


def _sb_decode(q, k_new, v_new, cache_k, cache_v, *, dec, tk=256):
    nd, rows, hd = cache_k.shape
    past = rows // SB_HEADS
    assert past % tk == 0 and hd == SB_HEAD_DIM
    width = SB_HEADS * hd
    new = pl.BlockSpec((dec, width), lambda b: (b, 0))
    cache = pl.BlockSpec((None, rows, hd), lambda b: (b, 0, 0))
    return pl.pallas_call(
        functools.partial(_sb_decode_kernel, dec=dec, tk=tk),
        out_shape=jax.ShapeDtypeStruct((nd * dec, width), BF16),
        grid=(nd,),
        in_specs=[new, new, new, cache, cache],
        out_specs=new,
        compiler_params=_params("parallel"),
        name="sb_decode",
    )(q, k_new, v_new, cache_k, cache_v)


def _mla_attn_kernel(q_ref, kt_ref, v_ref, o_ref, s_scr, p_scr, acc_scr, m_scr, *, t):
    i = pl.program_id(2)
    ones = jnp.ones((t, V_HEAD_DIM), BF16)
    heads = range(q_ref.shape[0])

    def scores(h, j):
        return _dot(q_ref[h], kt_ref[h, j])

    def weighted_values(h, p, j):
        v_ext = jnp.concatenate([v_ref[h, pl.ds(pl.multiple_of(j * t, t), t), :], ones], axis=1)
        return _dot(p, v_ext)

    def step(j, a, diagonal=False):
        for h in heads:
            if not diagonal:
                s_scr[h, 1 - a] = scores(h, j + 1)
            pv = weighted_values(h, p_scr[h, 1 - a], jnp.maximum(j - 1, 0))
            s = s_scr[h, a]
            if diagonal:
                qchunk = lax.broadcasted_iota(jnp.int32, (t, t), 0) // CHUNK
                kchunk = lax.broadcasted_iota(jnp.int32, (t, t), 1) // CHUNK
                s = jnp.where(kchunk <= qchunk, s, NEG_INF)
            m = m_scr[h]
            m_new = jnp.maximum(m, jnp.max(s, axis=1, keepdims=True))
            p = jnp.exp2(s - m_new).astype(BF16)
            acc = jnp.exp2(m - m_new) * (acc_scr[h] + pv)
            if diagonal:
                acc = acc + weighted_values(h, p, j)
                o = acc[:, :V_HEAD_DIM] / acc[:, V_HEAD_DIM:]
                o_ref[:, h * V_HEAD_DIM:(h + 1) * V_HEAD_DIM] = o.astype(o_ref.dtype)
            else:
                m_scr[h] = m_new
                p_scr[h, a] = p
                acc_scr[h] = acc

    for h in heads:
        s_scr[h, 0] = scores(h, 0)
        p_scr[h, 1] = jnp.zeros((t, t), BF16)
    acc_scr[...] = jnp.zeros_like(acc_scr)
    m_scr[...] = jnp.full(m_scr.shape, NEG_INF, F32)

    def pair(jj, _):
        step(2 * jj, 0)
        step(2 * jj + 1, 1)
        return 0

    lax.fori_loop(0, i // 2, pair, 0)
    odd = lax.rem(i, 2) == 1

    @pl.when(odd)
    def _():
        step(i - 1, 0)
        step(i, 1, diagonal=True)

    @pl.when(jnp.logical_not(odd))
    def _():
        step(i, 0, diagonal=True)


def _mla_attn(q_cat, kt, v, *, n_batch, seq, hb=2):
    t = kt.shape[3]
    assert seq % t == 0 and t % CHUNK == 0 and MLA_HEADS % hb == 0
    nq = seq // t
    return pl.pallas_call(
        functools.partial(_mla_attn_kernel, t=t),
        out_shape=jax.ShapeDtypeStruct((n_batch * seq, MLA_HEADS * V_HEAD_DIM), BF16),
        grid=(n_batch, MLA_HEADS // hb, nq),
        in_specs=[pl.BlockSpec((hb, t, QK_DIM), lambda b, g, i: (g, b * nq + i, 0)),
                  pl.BlockSpec((hb, nq, QK_DIM, t), lambda b, g, i: (g, b, 0, 0)),
                  pl.BlockSpec((hb, seq, V_HEAD_DIM), lambda b, g, i: (g, b, 0))],
        out_specs=pl.BlockSpec((t, hb * V_HEAD_DIM), lambda b, g, i: (b * nq + i, g)),
        scratch_shapes=[pltpu.VMEM((hb, 2, t, t), F32), pltpu.VMEM((hb, 2, t, t), BF16),
                        pltpu.VMEM((hb, t, 2 * V_HEAD_DIM), F32), pltpu.VMEM((hb, t, 1), F32)],
        compiler_params=_params("parallel", "parallel", "arbitrary"),
        name="mla_attn",
    )(q_cat, kt, v)


def _mla_decode_kernel(q_ref, wuk_ref, wuv_ref, cckv_ref, ckr_ref, nckv_ref, nkr_ref, o_ref, *, past, dec):
    rows = MLA_HEADS * dec
    ql = jnp.concatenate([_dot(q_ref[h, :, 0:QK_NOPE_DIM], wuk_ref[h]) for h in range(MLA_HEADS)], axis=0)
    ql = ql.astype(BF16)
    qr = jnp.concatenate([q_ref[h, :, QK_NOPE_DIM:QK_DIM] for h in range(MLA_HEADS)], axis=0)
    kc = cckv_ref[...].astype(BF16)
    kn = nckv_ref[...].astype(BF16)
    s_c = _dot_nt(ql, kc) + _dot_nt(qr, ckr_ref[...].astype(BF16))
    s_n = _dot_nt(ql, kn) + _dot_nt(qr, nkr_ref[...].astype(BF16))
    qpos = past + lax.rem(lax.broadcasted_iota(jnp.int32, (rows, dec), 0), dec)
    kpos = past + lax.broadcasted_iota(jnp.int32, (rows, dec), 1)
    s_n = jnp.where(kpos // CHUNK <= qpos // CHUNK, s_n, NEG_INF)
    m = jnp.maximum(jnp.max(s_c, axis=1, keepdims=True), jnp.max(s_n, axis=1, keepdims=True))
    p_c = jnp.exp2(s_c - m)
    p_n = jnp.exp2(s_n - m)
    l = jnp.sum(p_c, axis=1, keepdims=True) + jnp.sum(p_n, axis=1, keepdims=True)
    ol = (_dot(p_c.astype(BF16), kc) + _dot(p_n.astype(BF16), kn)) / l
    ol = ol.astype(BF16)
    for h in range(MLA_HEADS):
        o_h = _dot(ol[h * dec:(h + 1) * dec, :], wuv_ref[h])
        o_ref[:, h * V_HEAD_DIM:(h + 1) * V_HEAD_DIM] = o_h.astype(o_ref.dtype)


def _mla_decode(q_cat, w_uk, w_uv, cache_ckv, cache_kr, ckv, kr, *, q_blk0, dec):
    nd, past, kv_lora = cache_ckv.shape
    const3 = lambda b: (0, 0, 0)
    return pl.pallas_call(
        functools.partial(_mla_decode_kernel, past=past, dec=dec),
        out_shape=jax.ShapeDtypeStruct((nd * dec, MLA_HEADS * V_HEAD_DIM), BF16),
        grid=(nd,),
        in_specs=[pl.BlockSpec((MLA_HEADS, dec, QK_DIM), lambda b: (0, q_blk0 + b, 0)),
                  pl.BlockSpec(w_uk.shape, const3),
                  pl.BlockSpec(w_uv.shape, const3),
                  pl.BlockSpec((None, past, kv_lora), lambda b: (b, 0, 0)),
                  pl.BlockSpec((None, past, ROPE_DIM), lambda b: (b, 0, 0)),
                  pl.BlockSpec((dec, kv_lora), lambda b: (q_blk0 + b, 0)),
                  pl.BlockSpec((dec, ROPE_DIM), lambda b: (q_blk0 + b, 0))],
        out_specs=pl.BlockSpec((dec, MLA_HEADS * V_HEAD_DIM), lambda b: (b, 0)),
        compiler_params=_params("parallel"),
        name="mla_decode",
    )(q_cat, w_uk, w_uv, cache_ckv, cache_kr, ckv, kr)


def _merge_kernel(x_ref, osb_ref, omla_ref, ga_ref, gb_ref, wa_ref, wb_ref, wo_ref, g_ref, b_ref, *rest,
                  alpha, n_own):
    o_ref, u_ref = rest[-2:]
    i = pl.program_id(0)
    j = pl.program_id(1)
    last = j == pl.num_programs(1) - 1

    @pl.when(i < n_own)
    def _():
        u = ga_ref[...] * _dot(osb_ref[...], wa_ref[...]) + gb_ref[...] * _dot(omla_ref[...], wb_ref[...])
        u_ref[j] = u.astype(BF16)

    @pl.when(jnp.logical_and(last, i < n_own))
    def _():
        u_all = jnp.concatenate([u_ref[c] for c in range(u_ref.shape[0])], axis=1)
        mix = _dot(u_all, wo_ref[...])
        o_ref[...] = _layer_norm(alpha * x_ref[...] + mix, g_ref[...], b_ref[...])

    @pl.when(jnp.logical_and(last, i >= n_own))
    def _():
        o_ref[...] = jnp.zeros_like(o_ref)


def _merge(x, o_sb, o_mla, gates, w_br_a, w_br_b, w_o, ln_g, ln_b, *, alpha, out_rows, row0=0, into=None,
           tm=512, tn=512):
    m, d = x.shape
    nj = d // tn
    blk0 = row0 // tm
    n_own = m // tm
    n_rows_tiles = n_own if into is not None else out_rows // tm
    own = lambda i: jnp.minimum(i, n_own - 1)
    extra_specs = [] if into is None else [pl.BlockSpec(memory_space=pl.ANY)]
    extra_args = [] if into is None else [into]
    return pl.pallas_call(
        functools.partial(_merge_kernel, alpha=alpha, n_own=n_own),
        out_shape=jax.ShapeDtypeStruct((out_rows, d), F32),
        grid=(n_rows_tiles, nj),
        input_output_aliases={} if into is None else {10: 0},
        in_specs=[pl.BlockSpec((tm, d), lambda i, j: (own(i), 0)),
                  pl.BlockSpec((tm, o_sb.shape[1]), lambda i, j: (own(i), 0)),
                  pl.BlockSpec((tm, o_mla.shape[1]), lambda i, j: (own(i), 0)),
                  pl.BlockSpec((tm, tn), lambda i, j: (own(i), j)),
                  pl.BlockSpec((tm, tn), lambda i, j: (own(i), nj + j)),
                  pl.BlockSpec((w_br_a.shape[0], tn), lambda i, j: (0, j)),
                  pl.BlockSpec((w_br_b.shape[0], tn), lambda i, j: (0, j)),
                  pl.BlockSpec((d, d), lambda i, j: (0, 0), pipeline_mode=pl.Buffered(1)),
                  pl.BlockSpec((1, d), lambda i, j: (0, 0)),
                  pl.BlockSpec((1, d), lambda i, j: (0, 0))] + extra_specs,
        out_specs=pl.BlockSpec((tm, d), lambda i, j: (blk0 + i, 0)),
        scratch_shapes=[pltpu.VMEM((nj, tm, tn), BF16)],
        compiler_params=_params("parallel", "arbitrary"),
        name="merge_ln1",
    )(x, o_sb, o_mla, gates, gates, w_br_a, w_br_b, w_o, ln_g, ln_b, *extra_args)


def _router_kernel(x_ref, wh_ref, wl_ref, o_ref):
    x = x_ref[...]
    xh = x.astype(BF16)
    xl = (x - xh.astype(F32)).astype(BF16)
    logits = _dot(xh, wh_ref[...]) + (_dot(xh, wl_ref[...]) + _dot(xl, wh_ref[...]))
    lane = lax.broadcasted_iota(jnp.int32, logits.shape, 1)
    big = jnp.int32(LANES)
    is_grp = lane < N_GROUPS
    gl = jnp.where(is_grp, logits, -jnp.inf)
    gmax = jnp.max(gl, axis=1, keepdims=True)
    gsum = jnp.sum(jnp.where(is_grp, jnp.exp(gl - gmax), 0.0), axis=1, keepdims=True)
    p_grp = 1.0 / gsum
    grp = jnp.min(jnp.where(jnp.logical_and(is_grp, gl == gmax), lane, big), axis=1, keepdims=True)
    lo = N_GROUPS + grp * EXPERTS_PER_GROUP
    in_grp = jnp.logical_and(lane >= lo, lane < lo + EXPERTS_PER_GROUP)
    el = jnp.where(in_grp, logits, -jnp.inf)
    v1 = jnp.max(el, axis=1, keepdims=True)
    i1 = jnp.min(jnp.where(el == v1, lane, big), axis=1, keepdims=True)
    el2 = jnp.where(lane == i1, -jnp.inf, el)
    v2 = jnp.max(el2, axis=1, keepdims=True)
    i2 = jnp.min(jnp.where(el2 == v2, lane, big), axis=1, keepdims=True)
    e2w = jnp.exp(v2 - v1)
    w1 = p_grp / (1.0 + e2w)
    w2 = p_grp * e2w / (1.0 + e2w)
    out = jnp.where(lane == 0, (i1 - N_GROUPS).astype(F32),
                    jnp.where(lane == 1, (i2 - N_GROUPS).astype(F32),
                              jnp.where(lane == 2, w1, jnp.where(lane == 3, w2, 0.0))))
    o_ref[...] = out


def _router(x1, w_hi, w_lo, tm=ROW_TILE):
    m, d = x1.shape
    return pl.pallas_call(
        _router_kernel,
        out_shape=jax.ShapeDtypeStruct((m, LANES), F32),
        grid=(m // tm,),
        in_specs=[pl.BlockSpec((tm, d), lambda i: (i, 0)),
                  pl.BlockSpec((d, LANES), lambda i: (0, 0)),
                  pl.BlockSpec((d, LANES), lambda i: (0, 0))],
        out_specs=pl.BlockSpec((tm, LANES), lambda i: (i, 0)),
        compiler_params=_params("parallel"),
        name="router",
    )(x1, w_hi, w_lo)


def _row_gather_start(idx_ref, src_hbm, dst_ref, sem, n_rows, unrolled=False):
    def start(r):
        tok = idx_ref[0, 0, r]
        pltpu.make_async_copy(src_hbm.at[pl.ds(tok, 1)], dst_ref.at[pl.ds(r, 1)], sem).start()

    if unrolled:
        for r in range(n_rows):
            start(r)
    else:
        def body(r, _):
            start(r)
            return 0
        lax.fori_loop(0, n_rows, body, 0)


def _row_gather_wait(src_hbm, dst_ref, sem, n_rows):
    pltpu.make_async_copy(src_hbm.at[pl.ds(0, n_rows)], dst_ref, sem).wait()


GATHER_RING = 3
COL_CHUNK = 512


def _moe_kernel(te_ref, nu_ref, idx_ref, idx1_ref, idx2_ref, x_hbm, wg_ref, wu_ref, wd_ref, rw_ref,
                y_ref, xbuf0, xbuf1, xbuf2, sem, wgb, wub, wdb, *, tm):
    t = pl.program_id(0)
    n_used = nu_ref[0]
    bufs = (xbuf0, xbuf1, xbuf2)

    @pl.when(t == 0)
    def _():
        _row_gather_start(idx_ref, x_hbm, xbuf0, sem.at[0], tm)
        _row_gather_start(idx1_ref, x_hbm, xbuf1, sem.at[1], tm)

    @pl.when(jnp.logical_or(t == 0, te_ref[t] != te_ref[jnp.maximum(t - 1, 0)]))
    def _():
        wgb[...] = wg_ref[...].astype(BF16)
        wub[...] = wu_ref[...].astype(BF16)
        wdb[...] = wd_ref[...].astype(BF16)

    def run(a):
        ahead = (a + 2) % GATHER_RING

        @pl.when(t <= n_used + 1)
        def _():
            _row_gather_wait(x_hbm, bufs[a], sem.at[a], tm)

        @pl.when(t < n_used)
        def _():
            _row_gather_start(idx2_ref, x_hbm, bufs[ahead], sem.at[ahead], tm, unrolled=True)
            xb = bufs[a][...].astype(BF16)
            g = _dot(xb, wgb[...])
            u = _dot(xb, wub[...])
            h = (g / (1.0 + jnp.exp(-g))) * u
            y_ref[...] = _dot(h.astype(BF16), wdb[...]) * rw_ref[...]

    slot = lax.rem(t, GATHER_RING)
    for a in range(GATHER_RING):
        pl.when(slot == a)(functools.partial(run, a))

    @pl.when(t >= n_used)
    def _():
        y_ref[...] = jnp.zeros_like(y_ref)


def _moe(tile_expert, n_used, row_token, x1, w_gate, w_up, w_down, row_w, *, tm):
    n_tiles = tile_expert.shape[0]
    d = x1.shape[1]
    f = w_gate.shape[2]
    idx3 = row_token.reshape(n_tiles, 1, tm)
    wmap = lambda t, te, nu: (te[t], 0, 0)
    ahead = lambda k: pl.BlockSpec((1, 1, tm), lambda t, te, nu: (jnp.minimum(t + k, n_tiles - 1), 0, 0),
                                   memory_space=pltpu.SMEM)
    grid_spec = pltpu.PrefetchScalarGridSpec(
        num_scalar_prefetch=2,
        grid=(n_tiles,),
        in_specs=[ahead(0), ahead(1), ahead(2),
                  pl.BlockSpec(memory_space=pl.ANY),
                  pl.BlockSpec((None, d, f), wmap),
                  pl.BlockSpec((None, d, f), wmap),
                  pl.BlockSpec((None, f, d), wmap),
                  pl.BlockSpec((tm, 1), lambda t, te, nu: (t, 0))],
        out_specs=pl.BlockSpec((tm, d), lambda t, te, nu: (t, 0)),
        scratch_shapes=[pltpu.VMEM((tm, d), F32) for _ in range(GATHER_RING)] + [
                        pltpu.SemaphoreType.DMA((GATHER_RING,)),
                        pltpu.VMEM((d, f), BF16),
                        pltpu.VMEM((d, f), BF16),
                        pltpu.VMEM((f, d), BF16)],
    )
    return pl.pallas_call(
        functools.partial(_moe_kernel, tm=tm),
        out_shape=jax.ShapeDtypeStruct((n_tiles * tm, d), F32),
        grid_spec=grid_spec,
        compiler_params=_params("arbitrary"),
        name="moe_experts",
    )(tile_expert, n_used, idx3, idx3, idx3, x1, w_gate, w_up, w_down, row_w)


def _final_kernel(p1_ref, p2_ref, p1a_ref, p2a_ref, p1b_ref, p2b_ref, y_hbm, x1_ref, pe_ref, wpp_ref, wpg_ref,
                  g_ref, b_ref, o_ref, ya0, yb0, ya1, yb1, ya2, yb2, sem, *, tm, alpha):
    t = pl.program_id(0)
    last = pl.num_programs(0) - 1
    bufs = ((ya0, yb0), (ya1, yb1), (ya2, yb2))

    def start(pa_ref, pb_ref, s, unrolled):
        _row_gather_start(pa_ref, y_hbm, bufs[s][0], sem.at[s], tm, unrolled=unrolled)
        _row_gather_start(pb_ref, y_hbm, bufs[s][1], sem.at[s], tm, unrolled=unrolled)

    def wait(s):
        _row_gather_wait(y_hbm, bufs[s][0], sem.at[s], tm)
        _row_gather_wait(y_hbm, bufs[s][1], sem.at[s], tm)

    @pl.when(t == 0)
    def _():
        start(p1_ref, p2_ref, 0, False)
        start(p1a_ref, p2a_ref, 1, False)

    def run(a):
        wait(a)
        start(p1b_ref, p2b_ref, (a + 2) % GATHER_RING, True)
        f = bufs[a][0][...] + bufs[a][1][...]
        x2 = _layer_norm(alpha * x1_ref[...] + f, g_ref[...], b_ref[...])
        o_ref[...] = x2
        x2b = x2.astype(BF16)
        peb = pe_ref[...].astype(BF16)
        n = o_ref.shape[1]
        for c in range(0, n, COL_CHUNK):
            gate = 1.0 / (1.0 + jnp.exp(-_dot(x2b, wpg_ref[:, c:c + COL_CHUNK])))
            o_ref[:, c:c + COL_CHUNK] += gate * _dot(peb, wpp_ref[:, c:c + COL_CHUNK])

        @pl.when(t == last)
        def _():
            wait((a + 1) % GATHER_RING)
            wait((a + 2) % GATHER_RING)

    slot = lax.rem(t, GATHER_RING)
    for a in range(GATHER_RING):
        pl.when(slot == a)(functools.partial(run, a))


def _final(pos1, pos2, y_sorted, x1, x1_blk0, pe, w_pp, w_pg, ln_g, ln_b, *, alpha, tm=ROW_TILE):
    d = x1.shape[1]
    m = pe.shape[0]
    n_tiles = m // tm
    p1 = pos1.reshape(n_tiles, 1, tm)
    p2 = pos2.reshape(n_tiles, 1, tm)
    ahead = lambda k: pl.BlockSpec((1, 1, tm), lambda t: (jnp.minimum(t + k, n_tiles - 1), 0, 0),
                                   memory_space=pltpu.SMEM)
    const = lambda t: (0, 0)
    return pl.pallas_call(
        functools.partial(_final_kernel, tm=tm, alpha=alpha),
        out_shape=jax.ShapeDtypeStruct((m, d), F32),
        grid=(n_tiles,),
        in_specs=[ahead(0), ahead(0), ahead(1), ahead(1), ahead(2), ahead(2),
                  pl.BlockSpec(memory_space=pl.ANY),
                  pl.BlockSpec((tm, d), lambda t: (x1_blk0 + t, 0)),
                  pl.BlockSpec((tm, pe.shape[1]), lambda t: (t, 0)),
                  pl.BlockSpec(w_pp.shape, const),
                  pl.BlockSpec(w_pg.shape, const),
                  pl.BlockSpec((1, d), const),
                  pl.BlockSpec((1, d), const)],
        out_specs=pl.BlockSpec((tm, d), lambda t: (t, 0)),
        scratch_shapes=[pltpu.VMEM((tm, d), F32) for _ in range(2 * GATHER_RING)] + [
                        pltpu.SemaphoreType.DMA((GATHER_RING,))],
        compiler_params=_params("arbitrary"),
        name="combine_ln2_ple",
    )(p1, p2, p1, p2, p1, p2, y_sorted, x1, pe, w_pp, w_pg, ln_g, ln_b)


def _rotate_half_cols(w):
    half = ROPE_DIM // 2
    return jnp.concatenate([-w[..., half:], w[..., :half]], axis=-1)


def _rope_tables(pos):
    inv_freq = 1.0 / (ROPE_THETA ** (jnp.arange(0, ROPE_DIM, 2, dtype=F32) / ROPE_DIM))
    ang = pos.astype(F32)[:, None] * inv_freq[None, :]
    ang = jnp.concatenate([ang, ang, ang, ang], axis=-1)
    return jnp.cos(ang), jnp.sin(ang)


def _route_tables(route, tm):
    m = route.shape[0]
    e_pair = jnp.concatenate([route[:, 0], route[:, 1]]).astype(jnp.int32)
    w_pair = jnp.concatenate([route[:, 2], route[:, 3]])
    tok = jnp.arange(m, dtype=jnp.int32)
    tok_pair = jnp.concatenate([tok, tok])
    onehot = (e_pair[:, None] == jnp.arange(N_EXPERTS, dtype=jnp.int32)[None, :]).astype(jnp.int32)
    csum = jnp.cumsum(onehot, axis=0)
    rank = jnp.sum(csum * onehot, axis=1) - 1
    counts = csum[-1]
    padded = ((counts + tm - 1) // tm) * tm
    pad_end = jnp.cumsum(padded)
    pad_off = pad_end - padded
    pos = pad_off[e_pair] + rank
    n_rows = (2 * m // tm + N_EXPERTS - 1 + GATHER_RING - 1) * tm
    payload = jnp.stack([tok_pair, lax.bitcast_convert_type(w_pair, jnp.int32)], axis=1)
    rows = jnp.zeros((n_rows, 2), jnp.int32).at[pos].set(payload)
    row_token = rows[:, 0]
    row_w = lax.bitcast_convert_type(rows[:, 1], F32)
    tile_start = jnp.arange(n_rows // tm, dtype=jnp.int32) * tm
    tile_expert = jnp.sum((pad_end[None, :] <= tile_start[:, None]).astype(jnp.int32), axis=1)
    tile_expert = jnp.minimum(tile_expert, N_EXPERTS - 1)
    n_used = (pad_end[-1] // tm).astype(jnp.int32).reshape(1)
    last_expert = tile_expert[jnp.maximum(n_used[0] - 1, 0)]
    tile_expert = jnp.where(tile_start // tm < n_used[0], tile_expert, last_expert).astype(jnp.int32)
    return tile_expert, n_used, row_token, row_w.reshape(n_rows, 1), pos[:m], pos[m:]


def kernel(x_prompt, x_sample, cache_sb_k, cache_sb_v, cache_mla_ckv, cache_mla_krope, p_prompt, p_sample,
           w_in, q_a_norm, kv_a_norm, w_uq, w_ukv, w_br_a, w_br_b, w_o, ln1_g, ln1_b, ln2_g, ln2_b,
           w_router_group, w_router_expert, w_exp_gate, w_exp_up, w_exp_down, w_ple_proj, w_ple_gate):
    depth = w_in.shape[0]
    assert depth == 1, "single trunk layer"
    nb, seq, d = x_prompt.shape
    nd, dec, _ = x_sample.shape
    past = cache_sb_k.shape[2]
    sbw = SB_HEADS * SB_HEAD_DIM
    q_lora = q_a_norm.shape[1]
    kv_lora = kv_a_norm.shape[1]
    alpha = (2.0 * depth) ** 0.25
    mp = nb * seq
    ms = nd * dec

    w0 = w_in[0]
    o1 = 3 * sbw
    o2 = o1 + q_lora
    o3 = o2 + kv_lora
    o4 = o3 + ROPE_DIM
    w_sb = w0[:, :o1].astype(BF16)
    w_kr = w0[:, o3:o4]
    w_lat = jnp.concatenate([w0[:, o1:o3], w_kr, _rotate_half_cols(w_kr)], axis=1).astype(BF16)
    w_g = w0[:, o4:].astype(BF16)
    wq = w_uq[0].reshape(q_lora, MLA_HEADS, QK_DIM)
    wq_r = wq[:, :, QK_NOPE_DIM:]
    w_uq_all = jnp.concatenate([wq[:, :, :QK_NOPE_DIM].reshape(q_lora, -1), wq_r.reshape(q_lora, -1),
                                _rotate_half_cols(wq_r).reshape(q_lora, -1)], axis=1).astype(BF16)
    wkv = w_ukv[0].reshape(kv_lora, MLA_HEADS, QK_NOPE_DIM + V_HEAD_DIM)
    w_ukv_all = jnp.concatenate([wkv[:, :, :QK_NOPE_DIM].reshape(kv_lora, -1),
                                 wkv[:, :, QK_NOPE_DIM:].reshape(kv_lora, -1)], axis=1).astype(BF16)
    w_r = jnp.concatenate([w_router_group[0], w_router_expert[0]], axis=1)
    w_r = jnp.pad(w_r, ((0, 0), (0, LANES - w_r.shape[1])))
    w_r_hi = w_r.astype(BF16)
    w_r_lo = (w_r - w_r_hi.astype(F32)).astype(BF16)

    w_uk = jnp.transpose(wkv[:, :, :QK_NOPE_DIM], (1, 2, 0)).astype(BF16)
    w_uv = jnp.transpose(wkv[:, :, QK_NOPE_DIM:], (1, 0, 2)).astype(BF16)
    w_a, w_b, w_out = w_br_a[0].astype(BF16), w_br_b[0].astype(BF16), w_o[0].astype(BF16)
    w_pp, w_pg = w_ple_proj[0].astype(BF16), w_ple_gate[0].astype(BF16)

    xp = x_prompt.reshape(mp, d)
    cos_p, sin_p = _rope_tables(jnp.arange(seq, dtype=jnp.int32))
    tile = 256
    sbq_p, sbk_p, sbv_p, sbk_b, vt_p = _sb_proj(xp, w_sb, seq=seq, tm=tile)
    q_cat_p, ckv_p, kr_p = _mla_q_proj(xp, w_lat, q_a_norm, kv_a_norm, w_uq_all, cos_p, sin_p)
    gates_p = _gates(xp, w_g)
    o_sb_p = _sb_attn(sbq_p, sbk_b, vt_p, n_batch=nb, tq_total=seq, q_off=0, tq=tile, tk=tile)
    kt_mla, v_mla = _mla_kv_proj(ckv_p, kr_p, w_ukv_all, tm=512)
    o_mla_p = _mla_attn(q_cat_p, kt_mla, v_mla, n_batch=nb, seq=seq)

    xs = x_sample.reshape(ms, d)
    cos_s, sin_s = _rope_tables(jnp.tile(past + jnp.arange(dec, dtype=jnp.int32), ROW_TILE // dec))
    sbq_s, sbk_s, sbv_s = _sb_proj(xs, w_sb)
    q_cat_s, ckv_s, kr_s = _mla_q_proj(xs, w_lat, q_a_norm, kv_a_norm, w_uq_all, cos_s, sin_s)
    gates_s = _gates(xs, w_g)
    o_sb_s = _sb_decode(sbq_s, sbk_s, sbv_s, cache_sb_k.reshape(nd, past * SB_HEADS, SB_HEAD_DIM),
                        cache_sb_v.reshape(nd, past * SB_HEADS, SB_HEAD_DIM), dec=dec)
    o_mla_s = _mla_decode(q_cat_s, w_uk, w_uv, cache_mla_ckv[0], cache_mla_krope[0], ckv_s, kr_s,
                          q_blk0=0, dec=dec)

    x1 = _merge(xp, o_sb_p, o_mla_p, gates_p, w_a, w_b, w_out, ln1_g, ln1_b, alpha=alpha, out_rows=mp + ms)
    x1 = _merge(xs, o_sb_s, o_mla_s, gates_s, w_a, w_b, w_out, ln1_g, ln1_b, alpha=alpha, out_rows=mp + ms,
                row0=mp, into=x1)
    route = _router(x1, w_r_hi, w_r_lo)
    tile_expert, n_used, row_token, row_w, pos1, pos2 = _route_tables(route, ROW_TILE)
    y_sorted = _moe(tile_expert, n_used, row_token, x1, w_exp_gate[0], w_exp_up[0], w_exp_down[0], row_w,
                    tm=ROW_TILE)
    y_p = _final(pos1[:mp], pos2[:mp], y_sorted, x1, 0, p_prompt[0].reshape(mp, -1), w_pp, w_pg,
                 ln2_g, ln2_b, alpha=alpha)
    y_s = _final(pos1[mp:], pos2[mp:], y_sorted, x1, mp // ROW_TILE, p_sample[0].reshape(ms, -1), w_pp, w_pg,
                 ln2_g, ln2_b, alpha=alpha)

    hs = (SB_HEADS, SB_HEAD_DIM)
    return (y_p.reshape(nb, seq, d), y_s.reshape(nd, dec, d),
            sbk_p.reshape((1, nb, seq) + hs), sbv_p.reshape((1, nb, seq) + hs),
            ckv_p.reshape(1, nb, seq, kv_lora), kr_p.reshape(1, nb, seq, ROPE_DIM),
            sbk_s.reshape((1, nd, dec) + hs), sbv_s.reshape((1, nd, dec) + hs),
            ckv_s.reshape(1, nd, dec, kv_lora), kr_s.reshape(1, nd, dec, ROPE_DIM))
```

```python
import functools

import jax
import jax.numpy as jnp
from jax import lax
from jax.experimental import pallas as pl
from jax.experimental.pallas import tpu as pltpu

F32 = jnp.float32
BF16 = jnp.bfloat16

SB_HEADS = 8
SB_HEAD_DIM = 128
MLA_HEADS = 16
QK_NOPE_DIM = 128
ROPE_DIM = 64
V_HEAD_DIM = 128
QK_DIM = QK_NOPE_DIM + ROPE_DIM
CHUNK = 64
N_GROUPS = 4
EXPERTS_PER_GROUP = 8
N_EXPERTS = N_GROUPS * EXPERTS_PER_GROUP
ROPE_THETA = 10000.0
LN_EPS = 1e-5
RMS_EPS = 1e-6
NEG_INF = -1e30
MLA_SCALE = QK_DIM ** -0.5 * 1.4426950408889634
SB_SCALE = SB_HEAD_DIM ** -0.5
SB_SKIP = 120.0

VMEM_LIMIT = 56 * 1024 * 1024
ROW_TILE = 256
LANES = 128


def _params(*sem):
    return pltpu.CompilerParams(dimension_semantics=sem, vmem_limit_bytes=VMEM_LIMIT)


def _dot(a, b):
    return jnp.dot(a, b, preferred_element_type=F32)


def _dot_nt(a, b):
    return lax.dot_general(a, b, (((1,), (1,)), ((), ())), preferred_element_type=F32)


def _rms(x, g):
    ms = jnp.mean(x * x, axis=-1, keepdims=True)
    return x * lax.rsqrt(ms + RMS_EPS) * g


def _layer_norm(x, g, b):
    mu = jnp.mean(x, axis=-1, keepdims=True)
    xc = x - mu
    var = jnp.mean(xc * xc, axis=-1, keepdims=True)
    return xc * lax.rsqrt(var + LN_EPS) * g + b


def _sb_proj_kernel(x_ref, w_ref, q_ref, k_ref, v_ref, *tile_refs, width):
    xb = x_ref[...].astype(BF16)
    q = _dot(xb, w_ref[:, 0:width])
    q_ref[...] = (q * SB_SCALE).astype(BF16)
    k = _dot(xb, w_ref[:, width:2 * width])
    k_ref[...] = k
    v = _dot(xb, w_ref[:, 2 * width:3 * width])
    v_ref[...] = v
    if tile_refs:
        kb_ref, vt_ref = tile_refs
        kb_ref[...] = k.astype(BF16)
        for h in range(SB_HEADS):
            vt_ref[h] = v[:, h * SB_HEAD_DIM:(h + 1) * SB_HEAD_DIM].T.astype(BF16)


def _sb_proj(x, w_sb, *, seq=None, tm=ROW_TILE):
    m, d = x.shape
    width = w_sb.shape[1] // 3
    row = lambda i: (i, 0)
    out_sds = lambda dt: jax.ShapeDtypeStruct((m, width), dt)
    out_shape = [out_sds(BF16), out_sds(F32), out_sds(F32)]
    out_specs = [pl.BlockSpec((tm, width), row) for _ in range(3)]
    if seq is not None:
        nt = seq // tm
        out_shape += [out_sds(BF16), jax.ShapeDtypeStruct((m // seq, SB_HEADS, nt, SB_HEAD_DIM, tm), BF16)]
        out_specs += [pl.BlockSpec((tm, width), row),
                      pl.BlockSpec((None, SB_HEADS, None, SB_HEAD_DIM, tm), lambda i: (i // nt, 0, i % nt, 0, 0))]
    return pl.pallas_call(
        functools.partial(_sb_proj_kernel, width=width),
        out_shape=tuple(out_shape),
        grid=(m // tm,),
        in_specs=[pl.BlockSpec((tm, d), row), pl.BlockSpec((d, 3 * width), lambda i: (0, 0))],
        out_specs=tuple(out_specs),
        compiler_params=_params("parallel"),
        name="sb_proj",
    )(x, w_sb)


def _mla_q_proj_kernel(x_ref, wlat_ref, gq_ref, gkv_ref, wuq_ref, cos_ref, sin_ref,
                       q_ref, ckv_ref, kr_ref, *, q_lora, kv_lora):
    xb = x_ref[...].astype(BF16)
    cos = cos_ref[...]
    sin = sin_ref[...]
    ckv = _dot(xb, wlat_ref[:, q_lora:q_lora + kv_lora])
    ckv_ref[...] = _rms(ckv, gkv_ref[...])
    kr2 = _dot(xb, wlat_ref[:, q_lora + kv_lora:])
    kr_ref[...] = kr2[:, :ROPE_DIM] * cos[:, :ROPE_DIM] + kr2[:, ROPE_DIM:] * sin[:, :ROPE_DIM]
    cq = _dot(xb, wlat_ref[:, 0:q_lora])
    cqn = _rms(cq, gq_ref[...]).astype(BF16)
    n_nope = MLA_HEADS * QK_NOPE_DIM
    n_rope = MLA_HEADS * ROPE_DIM
    for h in range(MLA_HEADS):
        qn = _dot(cqn, wuq_ref[:, h * QK_NOPE_DIM:(h + 1) * QK_NOPE_DIM])
        q_ref[h, :, 0:QK_NOPE_DIM] = (qn * MLA_SCALE).astype(BF16)
    for hp in range(MLA_HEADS // 2):
        lo = n_nope + hp * LANES
        qr = _dot(cqn, wuq_ref[:, lo:lo + LANES])
        qrr = _dot(cqn, wuq_ref[:, lo + n_rope:lo + n_rope + LANES])
        rot = ((qr * cos + qrr * sin) * MLA_SCALE).astype(BF16)
        q_ref[2 * hp, :, QK_NOPE_DIM:QK_DIM] = rot[:, :ROPE_DIM]
        q_ref[2 * hp + 1, :, QK_NOPE_DIM:QK_DIM] = rot[:, ROPE_DIM:]


def _mla_q_proj(x, w_lat, gq, gkv, w_uq_all, cos, sin, tm=ROW_TILE):
    m, d = x.shape
    q_lora, kv_lora = gq.shape[1], gkv.shape[1]
    n_pos_tiles = cos.shape[0] // tm
    row = lambda i: (i, 0)
    pos = lambda i: (i % n_pos_tiles, 0)
    const = lambda i: (0, 0)
    return pl.pallas_call(
        functools.partial(_mla_q_proj_kernel, q_lora=q_lora, kv_lora=kv_lora),
        out_shape=(jax.ShapeDtypeStruct((MLA_HEADS, m, QK_DIM), BF16),
                   jax.ShapeDtypeStruct((m, kv_lora), F32),
                   jax.ShapeDtypeStruct((m, ROPE_DIM), F32)),
        grid=(m // tm,),
        in_specs=[pl.BlockSpec((tm, d), row),
                  pl.BlockSpec(w_lat.shape, const),
                  pl.BlockSpec((1, q_lora), const),
                  pl.BlockSpec((1, kv_lora), const),
                  pl.BlockSpec(w_uq_all.shape, const),
                  pl.BlockSpec((tm, LANES), pos),
                  pl.BlockSpec((tm, LANES), pos)],
        out_specs=(pl.BlockSpec((MLA_HEADS, tm, QK_DIM), lambda i: (0, i, 0)),
                   pl.BlockSpec((tm, kv_lora), row),
                   pl.BlockSpec((tm, ROPE_DIM), row)),
        compiler_params=_params("parallel"),
        name="mla_q_proj",
    )(x, w_lat, gq, gkv, w_uq_all, cos, sin)


def _mla_kv_proj_kernel(ckv_ref, kr_ref, wukv_ref, kt_ref, v_ref):
    cb = ckv_ref[...].astype(BF16)
    kr = kr_ref[...]
    kr_t = jnp.concatenate([kr, kr], axis=1).T[0:ROPE_DIM, :].astype(BF16)
    n_k = MLA_HEADS * QK_NOPE_DIM
    for h in range(MLA_HEADS):
        kn = _dot(cb, wukv_ref[:, h * QK_NOPE_DIM:(h + 1) * QK_NOPE_DIM])
        kt_ref[h, 0:QK_NOPE_DIM, :] = kn.T.astype(BF16)
        kt_ref[h, QK_NOPE_DIM:QK_DIM, :] = kr_t
        vv = _dot(cb, wukv_ref[:, n_k + h * V_HEAD_DIM:n_k + (h + 1) * V_HEAD_DIM])
        v_ref[h] = vv.astype(BF16)


def _mla_kv_proj(ckv, kr, w_ukv_all, tm):
    m, kv_lora = ckv.shape
    row = lambda i: (i, 0)
    return pl.pallas_call(
        _mla_kv_proj_kernel,
        out_shape=(jax.ShapeDtypeStruct((MLA_HEADS, m // tm, QK_DIM, tm), BF16),
                   jax.ShapeDtypeStruct((MLA_HEADS, m, V_HEAD_DIM), BF16)),
        grid=(m // tm,),
        in_specs=[pl.BlockSpec((tm, kv_lora), row),
                  pl.BlockSpec((tm, ROPE_DIM), row),
                  pl.BlockSpec(w_ukv_all.shape, lambda i: (0, 0))],
        out_specs=(pl.BlockSpec((MLA_HEADS, None, QK_DIM, tm), lambda i: (0, i, 0, 0)),
                   pl.BlockSpec((MLA_HEADS, tm, V_HEAD_DIM), lambda i: (0, i, 0))),
        compiler_params=_params("parallel"),
        name="mla_kv_proj",
    )(ckv, kr, w_ukv_all)


def _gates_kernel(x_ref, w_ref, o_ref):
    z = _dot(x_ref[...].astype(BF16), w_ref[...])
    o_ref[...] = (1.0 / (1.0 + jnp.exp(-z))).astype(o_ref.dtype)


def _gates(x, w_g, tm=1024, tn=1024):
    m, d = x.shape
    n = w_g.shape[1]
    tm = min(tm, m)
    return pl.pallas_call(
        _gates_kernel,
        out_shape=jax.ShapeDtypeStruct((m, n), BF16),
        grid=(m // tm, n // tn),
        in_specs=[pl.BlockSpec((tm, d), lambda i, j: (i, 0)),
                  pl.BlockSpec((d, tn), lambda i, j: (0, j))],
        out_specs=pl.BlockSpec((tm, tn), lambda i, j: (i, j)),
        compiler_params=_params("parallel", "arbitrary"),
        name="gates",
    )(x, w_g)


def _sb_attn_kernel(q_ref, k_ref, vt_ref, o_ref, *, tq, tk, q_off, n_heads):
    i = pl.program_id(2)
    t0 = q_off + i * tq
    hd = SB_HEAD_DIM
    heads = range(n_heads)
    qs = [q_ref[:, h * hd:(h + 1) * hd] for h in heads]
    rel = lax.broadcasted_iota(jnp.int32, (tk, tq), 0) - lax.broadcasted_iota(jnp.int32, (tk, tq), 1)
    r = lax.broadcasted_iota(jnp.int32, (tk, tk), 0)
    c = lax.broadcasted_iota(jnp.int32, (tk, tk), 1)
    upper = (c >= r).astype(BF16)

    def cond(carry):
        j, runs, _ = carry
        lowest = functools.reduce(jnp.minimum, runs)
        return jnp.logical_and(j >= 0, jnp.min(lowest) < SB_SKIP)

    def body(carry):
        j, runs, accs = carry
        k0 = pl.multiple_of(j * tk, tk)
        mask = rel < (t0 - k0)
        new_runs, new_accs = [], []
        for h in heads:
            z = _dot_nt(k_ref[pl.ds(k0, tk), h * hd:(h + 1) * hd], qs[h])
            sp = jnp.maximum(z, 0.0) + jnp.log(1.0 + jnp.exp(-jnp.abs(z)))
            sp = jnp.where(mask, sp, 0.0)
            hi = sp.astype(BF16)
            lo = (sp - hi.astype(F32)).astype(BF16)
            cs = _dot(upper, hi) + _dot(upper, lo) + runs[h]
            a = jnp.where(mask, jnp.exp(z - cs), 0.0)
            new_accs.append(accs[h] + _dot(vt_ref[h, j], a.astype(BF16)))
            new_runs.append(cs[0:1, :])
        return j - 1, tuple(new_runs), tuple(new_accs)

    j_start = (t0 + tq - 2) // tk
    init = (j_start, tuple(jnp.zeros((1, tq), F32) for _ in heads),
            tuple(jnp.zeros((hd, tq), F32) for _ in heads))
    _, _, accs = lax.while_loop(cond, body, init)
    for h in heads:
        o_ref[:, h * hd:(h + 1) * hd] = accs[h].T.astype(o_ref.dtype)


def _sb_attn(q, k, vt, *, n_batch, tq_total, q_off, tq, tk, n_heads=4):
    tk_total = k.shape[0] // n_batch
    nq = tq_total // tq
    hw = n_heads * SB_HEAD_DIM
    return pl.pallas_call(
        functools.partial(_sb_attn_kernel, tq=tq, tk=tk, q_off=q_off, n_heads=n_heads),
        out_shape=jax.ShapeDtypeStruct((n_batch * tq_total, SB_HEADS * SB_HEAD_DIM), BF16),
        grid=(n_batch, SB_HEADS // n_heads, nq),
        in_specs=[pl.BlockSpec((tq, hw), lambda b, g, i: (b * nq + i, g)),
                  pl.BlockSpec((tk_total, hw), lambda b, g, i: (b, g)),
                  pl.BlockSpec((None, n_heads, tk_total // tk, SB_HEAD_DIM, tk),
                               lambda b, g, i: (b, g, 0, 0, 0))],
        out_specs=pl.BlockSpec((tq, hw), lambda b, g, i: (b * nq + i, g)),
        compiler_params=_params("parallel", "parallel", "arbitrary"),
        name="sb_attn",
    )(q, k, vt)


def _sb_decode_kernel(q_ref, kn_ref, vn_ref, kc_ref, vc_ref, o_ref, *, dec, tk):
    hd = SB_HEAD_DIM
    heads = range(SB_HEADS)
    rows = SB_HEADS * dec
    past = kc_ref.shape[0] // SB_HEADS

    def softplus(z):
        return jnp.maximum(z, 0.0) + jnp.log(1.0 + jnp.exp(-jnp.abs(z)))

    def lower(n):
        return (lax.broadcasted_iota(jnp.int32, (n, n), 0) >= lax.broadcasted_iota(jnp.int32, (n, n), 1)).astype(BF16)

    def reverse_cumsum(sp, tri):
        hi = sp.astype(BF16)
        lo = (sp - hi.astype(F32)).astype(BF16)
        return _dot(hi, tri) + _dot(lo, tri)

    def head_cols(x, h):
        return x[:, h * hd:(h + 1) * hd]

    qs = [head_cols(q_ref[...], h) for h in heads]

    kn = kn_ref[...].astype(BF16)
    vn = vn_ref[...].astype(BF16)
    z = jnp.concatenate([_dot_nt(qs[h], head_cols(kn, h)) for h in heads], axis=0)
    qt = lax.rem(lax.broadcasted_iota(jnp.int32, (rows, dec), 0), dec)
    mask = lax.broadcasted_iota(jnp.int32, (rows, dec), 1) < qt
    cs = reverse_cumsum(jnp.where(mask, softplus(z), 0.0), lower(dec))
    a = jnp.where(mask, jnp.exp(z - cs), 0.0).astype(BF16)
    acc0 = jnp.concatenate([_dot(a[h * dec:(h + 1) * dec, :], head_cols(vn, h)) for h in heads], axis=0)
    tri = lower(tk)

    def cond(carry):
        j, run, _ = carry
        return jnp.logical_and(j >= 0, jnp.min(run) < SB_SKIP)

    def body(carry):
        j, run, acc = carry
        r0 = pl.multiple_of(j * (tk * SB_HEADS), tk * SB_HEADS)

        def head_rows(ref, h):
            return ref[pl.ds(r0 + h, tk, stride=SB_HEADS), :].astype(BF16)

        z = jnp.concatenate([_dot_nt(qs[h], head_rows(kc_ref, h)) for h in heads], axis=0)
        cs = reverse_cumsum(softplus(z), tri) + run
        a = jnp.exp(z - cs).astype(BF16)
        pv = jnp.concatenate([_dot(a[h * dec:(h + 1) * dec, :], head_rows(vc_ref, h)) for h in heads], axis=0)
        return j - 1, cs[:, 0:1], acc + pv

    _, _, acc = lax.while_loop(cond, body, (past // tk - 1, cs[:, 0:1], acc0))
    for h in heads:
        o_ref[:, h * hd:(h + 1) * hd] = acc[h * dec:(h + 1) * dec, :].astype(o_ref.dtype)


def _sb_decode(q, k_new, v_new, cache_k, cache_v, *, dec, tk=256):
    nd, rows, hd = cache_k.shape
    past = rows // SB_HEADS
    assert past % tk == 0 and hd == SB_HEAD_DIM
    width = SB_HEADS * hd
    new = pl.BlockSpec((dec, width), lambda b: (b, 0))
    cache = pl.BlockSpec((None, rows, hd), lambda b: (b, 0, 0))
    return pl.pallas_call(
        functools.partial(_sb_decode_kernel, dec=dec, tk=tk),
        out_shape=jax.ShapeDtypeStruct((nd * dec, width), BF16),
        grid=(nd,),
        in_specs=[new, new, new, cache, cache],
        out_specs=new,
        compiler_params=_params("parallel"),
        name="sb_decode",
    )(q, k_new, v_new, cache_k, cache_v)


def _mla_attn_kernel(q_ref, kt_ref, v_ref, o_ref, s_scr, p_scr, acc_scr, m_scr, *, t):
    i = pl.program_id(2)
    ones = jnp.ones((t, V_HEAD_DIM), BF16)
    heads = range(q_ref.shape[0])

    def scores(h, j):
        return _dot(q_ref[h], kt_ref[h, j])

    def weighted_values(h, p, j):
        v_ext = jnp.concatenate([v_ref[h, pl.ds(pl.multiple_of(j * t, t), t), :], ones], axis=1)
        return _dot(p, v_ext)

    def step(j, a, diagonal=False):
        for h in heads:
            if not diagonal:
                s_scr[h, 1 - a] = scores(h, j + 1)
            pv = weighted_values(h, p_scr[h, 1 - a], jnp.maximum(j - 1, 0))
            s = s_scr[h, a]
            if diagonal:
                qchunk = lax.broadcasted_iota(jnp.int32, (t, t), 0) // CHUNK
                kchunk = lax.broadcasted_iota(jnp.int32, (t, t), 1) // CHUNK
                s = jnp.where(kchunk <= qchunk, s, NEG_INF)
            m = m_scr[h]
            m_new = jnp.maximum(m, jnp.max(s, axis=1, keepdims=True))
            p = jnp.exp2(s - m_new).astype(BF16)
            acc = jnp.exp2(m - m_new) * (acc_scr[h] + pv)
            if diagonal:
                acc = acc + weighted_values(h, p, j)
                o = acc[:, :V_HEAD_DIM] / acc[:, V_HEAD_DIM:]
                o_ref[:, h * V_HEAD_DIM:(h + 1) * V_HEAD_DIM] = o.astype(o_ref.dtype)
            else:
                m_scr[h] = m_new
                p_scr[h, a] = p
                acc_scr[h] = acc

    for h in heads:
        s_scr[h, 0] = scores(h, 0)
        p_scr[h, 1] = jnp.zeros((t, t), BF16)
    acc_scr[...] = jnp.zeros_like(acc_scr)
    m_scr[...] = jnp.full(m_scr.shape, NEG_INF, F32)

    def pair(jj, _):
        step(2 * jj, 0)
        step(2 * jj + 1, 1)
        return 0

    lax.fori_loop(0, i // 2, pair, 0)
    odd = lax.rem(i, 2) == 1

    @pl.when(odd)
    def _():
        step(i - 1, 0)
        step(i, 1, diagonal=True)

    @pl.when(jnp.logical_not(odd))
    def _():
        step(i, 0, diagonal=True)


def _mla_attn(q_cat, kt, v, *, n_batch, seq, hb=2):
    t = kt.shape[3]
    assert seq % t == 0 and t % CHUNK == 0 and MLA_HEADS % hb == 0
    nq = seq // t
    return pl.pallas_call(
        functools.partial(_mla_attn_kernel, t=t),
        out_shape=jax.ShapeDtypeStruct((n_batch * seq, MLA_HEADS * V_HEAD_DIM), BF16),
        grid=(n_batch, MLA_HEADS // hb, nq),
        in_specs=[pl.BlockSpec((hb, t, QK_DIM), lambda b, g, i: (g, b * nq + i, 0)),
                  pl.BlockSpec((hb, nq, QK_DIM, t), lambda b, g, i: (g, b, 0, 0)),
                  pl.BlockSpec((hb, seq, V_HEAD_DIM), lambda b, g, i: (g, b, 0))],
        out_specs=pl.BlockSpec((t, hb * V_HEAD_DIM), lambda b, g, i: (b * nq + i, g)),
        scratch_shapes=[pltpu.VMEM((hb, 2, t, t), F32), pltpu.VMEM((hb, 2, t, t), BF16),
                        pltpu.VMEM((hb, t, 2 * V_HEAD_DIM), F32), pltpu.VMEM((hb, t, 1), F32)],
        compiler_params=_params("parallel", "parallel", "arbitrary"),
        name="mla_attn",
    )(q_cat, kt, v)


def _mla_decode_kernel(q_ref, wuk_ref, wuv_ref, cckv_ref, ckr_ref, nckv_ref, nkr_ref, o_ref, *, past, dec):
    rows = MLA_HEADS * dec
    ql = jnp.concatenate([_dot(q_ref[h, :, 0:QK_NOPE_DIM], wuk_ref[h]) for h in range(MLA_HEADS)], axis=0)
    ql = ql.astype(BF16)
    qr = jnp.concatenate([q_ref[h, :, QK_NOPE_DIM:QK_DIM] for h in range(MLA_HEADS)], axis=0)
    kc = cckv_ref[...].astype(BF16)
    kn = nckv_ref[...].astype(BF16)
    s_c = _dot_nt(ql, kc) + _dot_nt(qr, ckr_ref[...].astype(BF16))
    s_n = _dot_nt(ql, kn) + _dot_nt(qr, nkr_ref[...].astype(BF16))
    qpos = past + lax.rem(lax.broadcasted_iota(jnp.int32, (rows, dec), 0), dec)
    kpos = past + lax.broadcasted_iota(jnp.int32, (rows, dec), 1)
    s_n = jnp.where(kpos // CHUNK <= qpos // CHUNK, s_n, NEG_INF)
    m = jnp.maximum(jnp.max(s_c, axis=1, keepdims=True), jnp.max(s_n, axis=1, keepdims=True))
    p_c = jnp.exp2(s_c - m)
    p_n = jnp.exp2(s_n - m)
    l = jnp.sum(p_c, axis=1, keepdims=True) + jnp.sum(p_n, axis=1, keepdims=True)
    ol = (_dot(p_c.astype(BF16), kc) + _dot(p_n.astype(BF16), kn)) / l
    ol = ol.astype(BF16)
    for h in range(MLA_HEADS):
        o_h = _dot(ol[h * dec:(h + 1) * dec, :], wuv_ref[h])
        o_ref[:, h * V_HEAD_DIM:(h + 1) * V_HEAD_DIM] = o_h.astype(o_ref.dtype)


def _mla_decode(q_cat, w_uk, w_uv, cache_ckv, cache_kr, ckv, kr, *, q_blk0, dec):
    nd, past, kv_lora = cache_ckv.shape
    const3 = lambda b: (0, 0, 0)
    return pl.pallas_call(
        functools.partial(_mla_decode_kernel, past=past, dec=dec),
        out_shape=jax.ShapeDtypeStruct((nd * dec, MLA_HEADS * V_HEAD_DIM), BF16),
        grid=(nd,),
        in_specs=[pl.BlockSpec((MLA_HEADS, dec, QK_DIM), lambda b: (0, q_blk0 + b, 0)),
                  pl.BlockSpec(w_uk.shape, const3),
                  pl.BlockSpec(w_uv.shape, const3),
                  pl.BlockSpec((None, past, kv_lora), lambda b: (b, 0, 0)),
                  pl.BlockSpec((None, past, ROPE_DIM), lambda b: (b, 0, 0)),
                  pl.BlockSpec((dec, kv_lora), lambda b: (q_blk0 + b, 0)),
                  pl.BlockSpec((dec, ROPE_DIM), lambda b: (q_blk0 + b, 0))],
        out_specs=pl.BlockSpec((dec, MLA_HEADS * V_HEAD_DIM), lambda b: (b, 0)),
        compiler_params=_params("parallel"),
        name="mla_decode",
    )(q_cat, w_uk, w_uv, cache_ckv, cache_kr, ckv, kr)


def _merge_kernel(x_ref, osb_ref, omla_ref, gates_ref, wa_ref, wb_ref, wo_ref, g_ref, b_ref, *rest,
                  alpha, n_own):
    o_ref, u_ref = rest[-2:]
    i = pl.program_id(0)
    d = o_ref.shape[1]

    @pl.when(i < n_own)
    def _():
        osb = osb_ref[...]
        omla = omla_ref[...]
        for c in range(0, d, COL_CHUNK):
            ga = gates_ref[:, c:c + COL_CHUNK].astype(F32)
            gb = gates_ref[:, d + c:d + c + COL_CHUNK].astype(F32)
            u = ga * _dot(osb, wa_ref[:, c:c + COL_CHUNK]) + gb * _dot(omla, wb_ref[:, c:c + COL_CHUNK])
            u_ref[:, c:c + COL_CHUNK] = u.astype(BF16)
        mix = _dot(u_ref[...], wo_ref[...])
        o_ref[...] = _layer_norm(alpha * x_ref[...] + mix, g_ref[...], b_ref[...])

    @pl.when(i >= n_own)
    def _():
        o_ref[...] = jnp.zeros_like(o_ref)


def _merge(x, o_sb, o_mla, gates, w_br_a, w_br_b, w_o, ln_g, ln_b, *, alpha, out_rows, row0=0, into=None,
           tm=ROW_TILE):
    m, d = x.shape
    blk0 = row0 // tm
    n_own = m // tm
    n_rows_tiles = n_own if into is not None else out_rows // tm
    own = lambda i: (jnp.minimum(i, n_own - 1), 0)
    const = lambda i: (0, 0)
    resident = lambda w: pl.BlockSpec(w.shape, const, pipeline_mode=pl.Buffered(1))
    extra_specs = [] if into is None else [pl.BlockSpec(memory_space=pl.ANY)]
    extra_args = [] if into is None else [into]
    return pl.pallas_call(
        functools.partial(_merge_kernel, alpha=alpha, n_own=n_own),
        out_shape=jax.ShapeDtypeStruct((out_rows, d), F32),
        grid=(n_rows_tiles,),
        input_output_aliases={} if into is None else {9: 0},
        in_specs=[pl.BlockSpec((tm, d), own),
                  pl.BlockSpec((tm, o_sb.shape[1]), own),
                  pl.BlockSpec((tm, o_mla.shape[1]), own),
                  pl.BlockSpec((tm, 2 * d), own),
                  resident(w_br_a), resident(w_br_b), resident(w_o),
                  pl.BlockSpec((1, d), const),
                  pl.BlockSpec((1, d), const)] + extra_specs,
        out_specs=pl.BlockSpec((tm, d), lambda i: (blk0 + i, 0)),
        scratch_shapes=[pltpu.VMEM((tm, d), BF16)],
        compiler_params=_params("parallel"),
        name="merge_ln1",
    )(x, o_sb, o_mla, gates, w_br_a, w_br_b, w_o, ln_g, ln_b, *extra_args)


def _router_kernel(x_ref, wh_ref, wl_ref, o_ref):
    x = x_ref[...]
    xh = x.astype(BF16)
    xl = (x - xh.astype(F32)).astype(BF16)
    logits = _dot(xh, wh_ref[...]) + (_dot(xh, wl_ref[...]) + _dot(xl, wh_ref[...]))
    lane = lax.broadcasted_iota(jnp.int32, logits.shape, 1)
    big = jnp.int32(LANES)
    is_grp = lane < N_GROUPS
    gl = jnp.where(is_grp, logits, -jnp.inf)
    gmax = jnp.max(gl, axis=1, keepdims=True)
    gsum = jnp.sum(jnp.where(is_grp, jnp.exp(gl - gmax), 0.0), axis=1, keepdims=True)
    p_grp = 1.0 / gsum
    grp = jnp.min(jnp.where(jnp.logical_and(is_grp, gl == gmax), lane, big), axis=1, keepdims=True)
    lo = N_GROUPS + grp * EXPERTS_PER_GROUP
    in_grp = jnp.logical_and(lane >= lo, lane < lo + EXPERTS_PER_GROUP)
    el = jnp.where(in_grp, logits, -jnp.inf)
    v1 = jnp.max(el, axis=1, keepdims=True)
    i1 = jnp.min(jnp.where(el == v1, lane, big), axis=1, keepdims=True)
    el2 = jnp.where(lane == i1, -jnp.inf, el)
    v2 = jnp.max(el2, axis=1, keepdims=True)
    i2 = jnp.min(jnp.where(el2 == v2, lane, big), axis=1, keepdims=True)
    e2w = jnp.exp(v2 - v1)
    w1 = p_grp / (1.0 + e2w)
    w2 = p_grp * e2w / (1.0 + e2w)
    out = jnp.where(lane == 0, (i1 - N_GROUPS).astype(F32),
                    jnp.where(lane == 1, (i2 - N_GROUPS).astype(F32),
                              jnp.where(lane == 2, w1, jnp.where(lane == 3, w2, 0.0))))
    o_ref[...] = out


def _router(x1, w_hi, w_lo, tm=ROW_TILE):
    m, d = x1.shape
    return pl.pallas_call(
        _router_kernel,
        out_shape=jax.ShapeDtypeStruct((m, LANES), F32),
        grid=(m // tm,),
        in_specs=[pl.BlockSpec((tm, d), lambda i: (i, 0)),
                  pl.BlockSpec((d, LANES), lambda i: (0, 0)),
                  pl.BlockSpec((d, LANES), lambda i: (0, 0))],
        out_specs=pl.BlockSpec((tm, LANES), lambda i: (i, 0)),
        compiler_params=_params("parallel"),
        name="router",
    )(x1, w_hi, w_lo)


def _row_gather_start(idx_ref, src_hbm, dst_ref, sem, n_rows, unrolled=False):
    def start(r):
        tok = idx_ref[0, 0, r]
        pltpu.make_async_copy(src_hbm.at[pl.ds(tok, 1)], dst_ref.at[pl.ds(r, 1)], sem).start()

    if unrolled:
        for r in range(n_rows):
            start(r)
    else:
        def body(r, _):
            start(r)
            return 0
        lax.fori_loop(0, n_rows, body, 0)


def _row_gather_wait(src_hbm, dst_ref, sem, n_rows):
    pltpu.make_async_copy(src_hbm.at[pl.ds(0, n_rows)], dst_ref, sem).wait()


GATHER_RING = 3
COL_CHUNK = 512


def _moe_kernel(te_ref, nu_ref, ws_ref, nx_ref, idx_ref, idx1_ref, idx2_ref, x_hbm, wg_hbm, wu_hbm, wd_hbm,
                rw_ref, y_ref, xbuf0, xbuf1, xbuf2, sem, wg_f, wu_f, wd_f, wsem, wgb, wub, wdb, *, tm):
    t = pl.program_id(0)
    n_used = nu_ref[0]
    bufs = (xbuf0, xbuf1, xbuf2)

    def weight_copies(e, slot):
        return [pltpu.make_async_copy(w_hbm.at[e], w_f.at[slot], wsem.at[slot])
                for w_hbm, w_f in ((wg_hbm, wg_f), (wu_hbm, wu_f), (wd_hbm, wd_f))]

    @pl.when(t == 0)
    def _():
        _row_gather_start(idx_ref, x_hbm, xbuf0, sem.at[0], tm)
        _row_gather_start(idx1_ref, x_hbm, xbuf1, sem.at[1], tm)
        for cp in weight_copies(te_ref[0], 0):
            cp.start()

    first_of_expert = jnp.logical_or(t == 0, te_ref[t] != te_ref[jnp.maximum(t - 1, 0)])

    @pl.when(jnp.logical_and(first_of_expert, t < n_used))
    def _():
        slot = ws_ref[t]
        for cp in weight_copies(te_ref[t], slot):
            cp.wait()

        @pl.when(nx_ref[t] >= 0)
        def _():
            for cp in weight_copies(nx_ref[t], 1 - slot):
                cp.start()

        wgb[...] = wg_f[slot].astype(BF16)
        wub[...] = wu_f[slot].astype(BF16)
        wdb[...] = wd_f[slot].astype(BF16)

    def run(a):
        ahead = (a + 2) % GATHER_RING

        @pl.when(t <= n_used + 1)
        def _():
            _row_gather_wait(x_hbm, bufs[a], sem.at[a], tm)

        @pl.when(t < n_used)
        def _():
            _row_gather_start(idx2_ref, x_hbm, bufs[ahead], sem.at[ahead], tm, unrolled=True)
            xb = bufs[a][...].astype(BF16)
            g = _dot(xb, wgb[...])
            u = _dot(xb, wub[...])
            h = (g / (1.0 + jnp.exp(-g))) * u
            y_ref[...] = _dot(h.astype(BF16), wdb[...]) * rw_ref[...]

    slot = lax.rem(t, GATHER_RING)
    for a in range(GATHER_RING):
        pl.when(slot == a)(functools.partial(run, a))

    @pl.when(t >= n_used)
    def _():
        y_ref[...] = jnp.zeros_like(y_ref)


def _moe(tile_expert, n_used, w_slot, next_expert, row_token, x1, w_gate, w_up, w_down, row_w, *, tm):
    n_tiles = tile_expert.shape[0]
    d = x1.shape[1]
    f = w_gate.shape[2]
    idx3 = row_token.reshape(n_tiles, 1, tm)
    ahead = lambda k: pl.BlockSpec((1, 1, tm), lambda t, *_: (jnp.minimum(t + k, n_tiles - 1), 0, 0),
                                   memory_space=pltpu.SMEM)
    hbm = pl.BlockSpec(memory_space=pl.ANY)
    grid_spec = pltpu.PrefetchScalarGridSpec(
        num_scalar_prefetch=4,
        grid=(n_tiles,),
        in_specs=[ahead(0), ahead(1), ahead(2), hbm, hbm, hbm, hbm,
                  pl.BlockSpec((tm, 1), lambda t, *_: (t, 0))],
        out_specs=pl.BlockSpec((tm, d), lambda t, *_: (t, 0)),
        scratch_shapes=[pltpu.VMEM((tm, d), F32) for _ in range(GATHER_RING)] + [
                        pltpu.SemaphoreType.DMA((GATHER_RING,)),
                        pltpu.VMEM((2, d, f), F32),
                        pltpu.VMEM((2, d, f), F32),
                        pltpu.VMEM((2, f, d), F32),
                        pltpu.SemaphoreType.DMA((2,)),
                        pltpu.VMEM((d, f), BF16),
                        pltpu.VMEM((d, f), BF16),
                        pltpu.VMEM((f, d), BF16)],
    )
    return pl.pallas_call(
        functools.partial(_moe_kernel, tm=tm),
        out_shape=jax.ShapeDtypeStruct((n_tiles * tm, d), F32),
        grid_spec=grid_spec,
        compiler_params=_params("arbitrary"),
        name="moe_experts",
    )(tile_expert, n_used, w_slot, next_expert, idx3, idx3, idx3, x1, w_gate, w_up, w_down, row_w)


def _final_kernel(p1_ref, p2_ref, p1a_ref, p2a_ref, p1b_ref, p2b_ref, y_hbm, x1_ref, pe_ref, wpp_ref, wpg_ref,
                  g_ref, b_ref, o_ref, ya0, yb0, ya1, yb1, ya2, yb2, sem, *, tm, alpha):
    t = pl.program_id(0)
    last = pl.num_programs(0) - 1
    bufs = ((ya0, yb0), (ya1, yb1), (ya2, yb2))

    def start(pa_ref, pb_ref, s, unrolled):
        _row_gather_start(pa_ref, y_hbm, bufs[s][0], sem.at[s], tm, unrolled=unrolled)
        _row_gather_start(pb_ref, y_hbm, bufs[s][1], sem.at[s], tm, unrolled=unrolled)

    def wait(s):
        _row_gather_wait(y_hbm, bufs[s][0], sem.at[s], tm)
        _row_gather_wait(y_hbm, bufs[s][1], sem.at[s], tm)

    @pl.when(t == 0)
    def _():
        start(p1_ref, p2_ref, 0, False)
        start(p1a_ref, p2a_ref, 1, False)

    def run(a):
        wait(a)
        start(p1b_ref, p2b_ref, (a + 2) % GATHER_RING, True)
        f = bufs[a][0][...] + bufs[a][1][...]
        x2 = _layer_norm(alpha * x1_ref[...] + f, g_ref[...], b_ref[...])
        o_ref[...] = x2
        x2b = x2.astype(BF16)
        peb = pe_ref[...].astype(BF16)
        n = o_ref.shape[1]
        for c in range(0, n, COL_CHUNK):
            gate = 1.0 / (1.0 + jnp.exp(-_dot(x2b, wpg_ref[:, c:c + COL_CHUNK])))
            o_ref[:, c:c + COL_CHUNK] += gate * _dot(peb, wpp_ref[:, c:c + COL_CHUNK])

        @pl.when(t == last)
        def _():
            wait((a + 1) % GATHER_RING)
            wait((a + 2) % GATHER_RING)

    slot = lax.rem(t, GATHER_RING)
    for a in range(GATHER_RING):
        pl.when(slot == a)(functools.partial(run, a))


def _final(pos1, pos2, y_sorted, x1, x1_blk0, pe, w_pp, w_pg, ln_g, ln_b, *, alpha, tm=ROW_TILE):
    d = x1.shape[1]
    m = pe.shape[0]
    n_tiles = m // tm
    p1 = pos1.reshape(n_tiles, 1, tm)
    p2 = pos2.reshape(n_tiles, 1, tm)
    ahead = lambda k: pl.BlockSpec((1, 1, tm), lambda t: (jnp.minimum(t + k, n_tiles - 1), 0, 0),
                                   memory_space=pltpu.SMEM)
    const = lambda t: (0, 0)
    return pl.pallas_call(
        functools.partial(_final_kernel, tm=tm, alpha=alpha),
        out_shape=jax.ShapeDtypeStruct((m, d), F32),
        grid=(n_tiles,),
        in_specs=[ahead(0), ahead(0), ahead(1), ahead(1), ahead(2), ahead(2),
                  pl.BlockSpec(memory_space=pl.ANY),
                  pl.BlockSpec((tm, d), lambda t: (x1_blk0 + t, 0)),
                  pl.BlockSpec((tm, pe.shape[1]), lambda t: (t, 0)),
                  pl.BlockSpec(w_pp.shape, const),
                  pl.BlockSpec(w_pg.shape, const),
                  pl.BlockSpec((1, d), const),
                  pl.BlockSpec((1, d), const)],
        out_specs=pl.BlockSpec((tm, d), lambda t: (t, 0)),
        scratch_shapes=[pltpu.VMEM((tm, d), F32) for _ in range(2 * GATHER_RING)] + [
                        pltpu.SemaphoreType.DMA((GATHER_RING,))],
        compiler_params=_params("arbitrary"),
        name="combine_ln2_ple",
    )(p1, p2, p1, p2, p1, p2, y_sorted, x1, pe, w_pp, w_pg, ln_g, ln_b)


def _rotate_half_cols(w):
    half = ROPE_DIM // 2
    return jnp.concatenate([-w[..., half:], w[..., :half]], axis=-1)


def _rope_tables(pos):
    inv_freq = 1.0 / (ROPE_THETA ** (jnp.arange(0, ROPE_DIM, 2, dtype=F32) / ROPE_DIM))
    ang = pos.astype(F32)[:, None] * inv_freq[None, :]
    ang = jnp.concatenate([ang, ang, ang, ang], axis=-1)
    return jnp.cos(ang), jnp.sin(ang)


def _route_tables(route, tm):
    m = route.shape[0]
    e_pair = jnp.concatenate([route[:, 0], route[:, 1]]).astype(jnp.int32)
    w_pair = jnp.concatenate([route[:, 2], route[:, 3]])
    tok = jnp.arange(m, dtype=jnp.int32)
    tok_pair = jnp.concatenate([tok, tok])
    onehot = (e_pair[:, None] == jnp.arange(N_EXPERTS, dtype=jnp.int32)[None, :]).astype(jnp.int32)
    csum = jnp.cumsum(onehot, axis=0)
    rank = jnp.sum(csum * onehot, axis=1) - 1
    counts = csum[-1]
    padded = ((counts + tm - 1) // tm) * tm
    pad_end = jnp.cumsum(padded)
    pad_off = pad_end - padded
    pos = pad_off[e_pair] + rank
    n_rows = (2 * m // tm + N_EXPERTS - 1 + GATHER_RING - 1) * tm
    payload = jnp.stack([tok_pair, lax.bitcast_convert_type(w_pair, jnp.int32)], axis=1)
    rows = jnp.zeros((n_rows, 2), jnp.int32).at[pos].set(payload)
    row_token = rows[:, 0]
    row_w = lax.bitcast_convert_type(rows[:, 1], F32)
    tile_start = jnp.arange(n_rows // tm, dtype=jnp.int32) * tm
    tile_expert = jnp.sum((pad_end[None, :] <= tile_start[:, None]).astype(jnp.int32), axis=1)
    tile_expert = jnp.minimum(tile_expert, N_EXPERTS - 1)
    n_used = (pad_end[-1] // tm).astype(jnp.int32).reshape(1)
    last_expert = tile_expert[jnp.maximum(n_used[0] - 1, 0)]
    tile_expert = jnp.where(tile_start // tm < n_used[0], tile_expert, last_expert).astype(jnp.int32)
    changed = jnp.concatenate([jnp.zeros((1,), jnp.int32),
                               (tile_expert[1:] != tile_expert[:-1]).astype(jnp.int32)])
    w_slot = (jnp.cumsum(changed) % 2).astype(jnp.int32)
    experts = jnp.arange(N_EXPERTS, dtype=jnp.int32)
    later_nonempty = jnp.logical_and(experts[None, :] > experts[:, None], counts[None, :] > 0)
    next_nonempty = jnp.min(jnp.where(later_nonempty, experts[None, :], N_EXPERTS), axis=1)
    next_nonempty = jnp.where(next_nonempty == N_EXPERTS, -1, next_nonempty).astype(jnp.int32)
    next_expert = next_nonempty[tile_expert]
    return tile_expert, n_used, w_slot, next_expert, row_token, row_w.reshape(n_rows, 1), pos[:m], pos[m:]


def kernel(x_prompt, x_sample, cache_sb_k, cache_sb_v, cache_mla_ckv, cache_mla_krope, p_prompt, p_sample,
           w_in, q_a_norm, kv_a_norm, w_uq, w_ukv, w_br_a, w_br_b, w_o, ln1_g, ln1_b, ln2_g, ln2_b,
           w_router_group, w_router_expert, w_exp_gate, w_exp_up, w_exp_down, w_ple_proj, w_ple_gate):
    depth = w_in.shape[0]
    assert depth == 1, "single trunk layer"
    nb, seq, d = x_prompt.shape
    nd, dec, _ = x_sample.shape
    past = cache_sb_k.shape[2]
    sbw = SB_HEADS * SB_HEAD_DIM
    q_lora = q_a_norm.shape[1]
    kv_lora = kv_a_norm.shape[1]
    alpha = (2.0 * depth) ** 0.25
    mp = nb * seq
    ms = nd * dec

    w0 = w_in[0]
    o1 = 3 * sbw
    o2 = o1 + q_lora
    o3 = o2 + kv_lora
    o4 = o3 + ROPE_DIM
    w_sb = w0[:, :o1].astype(BF16)
    w_kr = w0[:, o3:o4]
    w_lat = jnp.concatenate([w0[:, o1:o3], w_kr, _rotate_half_cols(w_kr)], axis=1).astype(BF16)
    w_g = w0[:, o4:].astype(BF16)
    wq = w_uq[0].reshape(q_lora, MLA_HEADS, QK_DIM)
    wq_r = wq[:, :, QK_NOPE_DIM:]
    w_uq_all = jnp.concatenate([wq[:, :, :QK_NOPE_DIM].reshape(q_lora, -1), wq_r.reshape(q_lora, -1),
                                _rotate_half_cols(wq_r).reshape(q_lora, -1)], axis=1).astype(BF16)
    wkv = w_ukv[0].reshape(kv_lora, MLA_HEADS, QK_NOPE_DIM + V_HEAD_DIM)
    w_ukv_all = jnp.concatenate([wkv[:, :, :QK_NOPE_DIM].reshape(kv_lora, -1),
                                 wkv[:, :, QK_NOPE_DIM:].reshape(kv_lora, -1)], axis=1).astype(BF16)
    w_r = jnp.concatenate([w_router_group[0], w_router_expert[0]], axis=1)
    w_r = jnp.pad(w_r, ((0, 0), (0, LANES - w_r.shape[1])))
    w_r_hi = w_r.astype(BF16)
    w_r_lo = (w_r - w_r_hi.astype(F32)).astype(BF16)

    w_uk = jnp.transpose(wkv[:, :, :QK_NOPE_DIM], (1, 2, 0)).astype(BF16)
    w_uv = jnp.transpose(wkv[:, :, QK_NOPE_DIM:], (1, 0, 2)).astype(BF16)
    w_a, w_b, w_out = w_br_a[0].astype(BF16), w_br_b[0].astype(BF16), w_o[0].astype(BF16)
    w_pp, w_pg = w_ple_proj[0].astype(BF16), w_ple_gate[0].astype(BF16)

    xp = x_prompt.reshape(mp, d)
    cos_p, sin_p = _rope_tables(jnp.arange(seq, dtype=jnp.int32))
    tile = 256
    sbq_p, sbk_p, sbv_p, sbk_b, vt_p = _sb_proj(xp, w_sb, seq=seq, tm=tile)
    q_cat_p, ckv_p, kr_p = _mla_q_proj(xp, w_lat, q_a_norm, kv_a_norm, w_uq_all, cos_p, sin_p)
    gates_p = _gates(xp, w_g)
    o_sb_p = _sb_attn(sbq_p, sbk_b, vt_p, n_batch=nb, tq_total=seq, q_off=0, tq=tile, tk=tile)
    kt_mla, v_mla = _mla_kv_proj(ckv_p, kr_p, w_ukv_all, tm=512)
    o_mla_p = _mla_attn(q_cat_p, kt_mla, v_mla, n_batch=nb, seq=seq)

    xs = x_sample.reshape(ms, d)
    cos_s, sin_s = _rope_tables(jnp.tile(past + jnp.arange(dec, dtype=jnp.int32), ROW_TILE // dec))
    sbq_s, sbk_s, sbv_s = _sb_proj(xs, w_sb)
    q_cat_s, ckv_s, kr_s = _mla_q_proj(xs, w_lat, q_a_norm, kv_a_norm, w_uq_all, cos_s, sin_s)
    gates_s = _gates(xs, w_g)
    o_sb_s = _sb_decode(sbq_s, sbk_s, sbv_s, cache_sb_k.reshape(nd, past * SB_HEADS, SB_HEAD_DIM),
                        cache_sb_v.reshape(nd, past * SB_HEADS, SB_HEAD_DIM), dec=dec)
    o_mla_s = _mla_decode(q_cat_s, w_uk, w_uv, cache_mla_ckv[0], cache_mla_krope[0], ckv_s, kr_s,
                          q_blk0=0, dec=dec)

    x1 = _merge(xp, o_sb_p, o_mla_p, gates_p, w_a, w_b, w_out, ln1_g, ln1_b, alpha=alpha, out_rows=mp + ms)
    x1 = _merge(xs, o_sb_s, o_mla_s, gates_s, w_a, w_b, w_out, ln1_g, ln1_b, alpha=alpha, out_rows=mp + ms,
                row0=mp, into=x1)
    route = _router(x1, w_r_hi, w_r_lo)
    tile_expert, n_used, w_slot, next_expert, row_token, row_w, pos1, pos2 = _route_tables(route, ROW_TILE)
    y_sorted = _moe(tile_expert, n_used, w_slot, next_expert, row_token, x1, w_exp_gate[0], w_exp_up[0],
                    w_exp_down[0], row_w, tm=ROW_TILE)
    y_p = _final(pos1[:mp], pos2[:mp], y_sorted, x1, 0, p_prompt[0].reshape(mp, -1), w_pp, w_pg,
                 ln2_g, ln2_b, alpha=alpha)
    y_s = _final(pos1[mp:], pos2[mp:], y_sorted, x1, mp // ROW_TILE, p_sample[0].reshape(ms, -1), w_pp, w_pg,
                 ln2_g, ln2_b, alpha=alpha)

    hs = (SB_HEADS, SB_HEAD_DIM)
    return (y_p.reshape(nb, seq, d), y_s.reshape(nd, dec, d),
            sbk_p.reshape((1, nb, seq) + hs), sbv_p.reshape((1, nb, seq) + hs),
            ckv_p.reshape(1, nb, seq, kv_lora), kr_p.reshape(1, nb, seq, ROPE_DIM),
            sbk_s.reshape((1, nd, dec) + hs), sbv_s.reshape((1, nd, dec) + hs),
            ckv_s.reshape(1, nd, dec, kv_lora), kr_s.reshape(1, nd, dec, ROPE_DIM))
```

```python
import functools

import jax
import jax.numpy as jnp
from jax import lax
from jax.experimental import pallas as pl
from jax.experimental.pallas import tpu as pltpu

F32 = jnp.float32
BF16 = jnp.bfloat16

SB_HEADS = 8
SB_HEAD_DIM = 128
MLA_HEADS = 16
QK_NOPE_DIM = 128
ROPE_DIM = 64
V_HEAD_DIM = 128
QK_DIM = QK_NOPE_DIM + ROPE_DIM
CHUNK = 64
N_GROUPS = 4
EXPERTS_PER_GROUP = 8
N_EXPERTS = N_GROUPS * EXPERTS_PER_GROUP
ROPE_THETA = 10000.0
LN_EPS = 1e-5
RMS_EPS = 1e-6
NEG_INF = -1e30
MLA_SCALE = QK_DIM ** -0.5 * 1.4426950408889634
SB_SCALE = SB_HEAD_DIM ** -0.5
SB_SKIP = 120.0

VMEM_LIMIT = 56 * 1024 * 1024
ROW_TILE = 256
LANES = 128


def _params(*sem):
    return pltpu.CompilerParams(dimension_semantics=sem, vmem_limit_bytes=VMEM_LIMIT)


def _dot(a, b):
    return jnp.dot(a, b, preferred_element_type=F32)


def _dot_nt(a, b):
    return lax.dot_general(a, b, (((1,), (1,)), ((), ())), preferred_element_type=F32)


def _rms(x, g):
    ms = jnp.mean(x * x, axis=-1, keepdims=True)
    return x * lax.rsqrt(ms + RMS_EPS) * g


def _layer_norm(x, g, b):
    mu = jnp.mean(x, axis=-1, keepdims=True)
    xc = x - mu
    var = jnp.mean(xc * xc, axis=-1, keepdims=True)
    return xc * lax.rsqrt(var + LN_EPS) * g + b


def _sb_proj_kernel(x_ref, w_ref, q_ref, k_ref, v_ref, *tile_refs, width):
    xb = x_ref[...].astype(BF16)
    q = _dot(xb, w_ref[:, 0:width])
    q_ref[...] = (q * SB_SCALE).astype(BF16)
    k = _dot(xb, w_ref[:, width:2 * width])
    k_ref[...] = k
    v = _dot(xb, w_ref[:, 2 * width:3 * width])
    v_ref[...] = v
    if tile_refs:
        kb_ref, vt_ref = tile_refs
        kb_ref[...] = k.astype(BF16)
        for h in range(SB_HEADS):
            vt_ref[h] = v[:, h * SB_HEAD_DIM:(h + 1) * SB_HEAD_DIM].T.astype(BF16)


def _sb_proj(x, w_sb, *, seq=None, tm=ROW_TILE):
    m, d = x.shape
    width = w_sb.shape[1] // 3
    row = lambda i: (i, 0)
    out_sds = lambda dt: jax.ShapeDtypeStruct((m, width), dt)
    out_shape = [out_sds(BF16), out_sds(F32), out_sds(F32)]
    out_specs = [pl.BlockSpec((tm, width), row) for _ in range(3)]
    if seq is not None:
        nt = seq // tm
        out_shape += [out_sds(BF16), jax.ShapeDtypeStruct((m // seq, SB_HEADS, nt, SB_HEAD_DIM, tm), BF16)]
        out_specs += [pl.BlockSpec((tm, width), row),
                      pl.BlockSpec((None, SB_HEADS, None, SB_HEAD_DIM, tm), lambda i: (i // nt, 0, i % nt, 0, 0))]
    return pl.pallas_call(
        functools.partial(_sb_proj_kernel, width=width),
        out_shape=tuple(out_shape),
        grid=(m // tm,),
        in_specs=[pl.BlockSpec((tm, d), row), pl.BlockSpec((d, 3 * width), lambda i: (0, 0))],
        out_specs=tuple(out_specs),
        compiler_params=_params("parallel"),
        name="sb_proj",
    )(x, w_sb)


def _mla_q_proj_kernel(x_ref, wlat_ref, gq_ref, gkv_ref, wuq_ref, cos_ref, sin_ref,
                       q_ref, ckv_ref, kr_ref, *, q_lora, kv_lora):
    xb = x_ref[...].astype(BF16)
    cos = cos_ref[...]
    sin = sin_ref[...]
    ckv = _dot(xb, wlat_ref[:, q_lora:q_lora + kv_lora])
    ckv_ref[...] = _rms(ckv, gkv_ref[...])
    kr2 = _dot(xb, wlat_ref[:, q_lora + kv_lora:])
    kr_ref[...] = kr2[:, :ROPE_DIM] * cos[:, :ROPE_DIM] + kr2[:, ROPE_DIM:] * sin[:, :ROPE_DIM]
    cq = _dot(xb, wlat_ref[:, 0:q_lora])
    cqn = _rms(cq, gq_ref[...]).astype(BF16)
    n_nope = MLA_HEADS * QK_NOPE_DIM
    n_rope = MLA_HEADS * ROPE_DIM
    for h in range(MLA_HEADS):
        qn = _dot(cqn, wuq_ref[:, h * QK_NOPE_DIM:(h + 1) * QK_NOPE_DIM])
        q_ref[h, :, 0:QK_NOPE_DIM] = (qn * MLA_SCALE).astype(BF16)
    for hp in range(MLA_HEADS // 2):
        lo = n_nope + hp * LANES
        qr = _dot(cqn, wuq_ref[:, lo:lo + LANES])
        qrr = _dot(cqn, wuq_ref[:, lo + n_rope:lo + n_rope + LANES])
        rot = ((qr * cos + qrr * sin) * MLA_SCALE).astype(BF16)
        q_ref[2 * hp, :, QK_NOPE_DIM:QK_DIM] = rot[:, :ROPE_DIM]
        q_ref[2 * hp + 1, :, QK_NOPE_DIM:QK_DIM] = rot[:, ROPE_DIM:]


def _mla_q_proj(x, w_lat, gq, gkv, w_uq_all, cos, sin, tm=ROW_TILE):
    m, d = x.shape
    q_lora, kv_lora = gq.shape[1], gkv.shape[1]
    n_pos_tiles = cos.shape[0] // tm
    row = lambda i: (i, 0)
    pos = lambda i: (i % n_pos_tiles, 0)
    const = lambda i: (0, 0)
    return pl.pallas_call(
        functools.partial(_mla_q_proj_kernel, q_lora=q_lora, kv_lora=kv_lora),
        out_shape=(jax.ShapeDtypeStruct((MLA_HEADS, m, QK_DIM), BF16),
                   jax.ShapeDtypeStruct((m, kv_lora), F32),
                   jax.ShapeDtypeStruct((m, ROPE_DIM), F32)),
        grid=(m // tm,),
        in_specs=[pl.BlockSpec((tm, d), row),
                  pl.BlockSpec(w_lat.shape, const),
                  pl.BlockSpec((1, q_lora), const),
                  pl.BlockSpec((1, kv_lora), const),
                  pl.BlockSpec(w_uq_all.shape, const),
                  pl.BlockSpec((tm, LANES), pos),
                  pl.BlockSpec((tm, LANES), pos)],
        out_specs=(pl.BlockSpec((MLA_HEADS, tm, QK_DIM), lambda i: (0, i, 0)),
                   pl.BlockSpec((tm, kv_lora), row),
                   pl.BlockSpec((tm, ROPE_DIM), row)),
        compiler_params=_params("parallel"),
        name="mla_q_proj",
    )(x, w_lat, gq, gkv, w_uq_all, cos, sin)


def _mla_kv_proj_kernel(ckv_ref, kr_ref, wukv_ref, kt_ref, v_ref):
    cb = ckv_ref[...].astype(BF16)
    kr = kr_ref[...]
    kr_t = jnp.concatenate([kr, kr], axis=1).T[0:ROPE_DIM, :].astype(BF16)
    n_k = MLA_HEADS * QK_NOPE_DIM
    for h in range(MLA_HEADS):
        kn = _dot(cb, wukv_ref[:, h * QK_NOPE_DIM:(h + 1) * QK_NOPE_DIM])
        kt_ref[h, 0:QK_NOPE_DIM, :] = kn.T.astype(BF16)
        kt_ref[h, QK_NOPE_DIM:QK_DIM, :] = kr_t
        vv = _dot(cb, wukv_ref[:, n_k + h * V_HEAD_DIM:n_k + (h + 1) * V_HEAD_DIM])
        v_ref[h] = vv.astype(BF16)


def _mla_kv_proj(ckv, kr, w_ukv_all, tm):
    m, kv_lora = ckv.shape
    row = lambda i: (i, 0)
    return pl.pallas_call(
        _mla_kv_proj_kernel,
        out_shape=(jax.ShapeDtypeStruct((MLA_HEADS, m // tm, QK_DIM, tm), BF16),
                   jax.ShapeDtypeStruct((MLA_HEADS, m, V_HEAD_DIM), BF16)),
        grid=(m // tm,),
        in_specs=[pl.BlockSpec((tm, kv_lora), row),
                  pl.BlockSpec((tm, ROPE_DIM), row),
                  pl.BlockSpec(w_ukv_all.shape, lambda i: (0, 0))],
        out_specs=(pl.BlockSpec((MLA_HEADS, None, QK_DIM, tm), lambda i: (0, i, 0, 0)),
                   pl.BlockSpec((MLA_HEADS, tm, V_HEAD_DIM), lambda i: (0, i, 0))),
        compiler_params=_params("parallel"),
        name="mla_kv_proj",
    )(ckv, kr, w_ukv_all)


def _gates_kernel(x_ref, w_ref, o_ref):
    z = _dot(x_ref[...].astype(BF16), w_ref[...])
    o_ref[...] = (1.0 / (1.0 + jnp.exp(-z))).astype(o_ref.dtype)


def _gates(x, w_g, tm=1024, tn=1024):
    m, d = x.shape
    n = w_g.shape[1]
    tm = min(tm, m)
    return pl.pallas_call(
        _gates_kernel,
        out_shape=jax.ShapeDtypeStruct((m, n), BF16),
        grid=(m // tm, n // tn),
        in_specs=[pl.BlockSpec((tm, d), lambda i, j: (i, 0)),
                  pl.BlockSpec((d, tn), lambda i, j: (0, j))],
        out_specs=pl.BlockSpec((tm, tn), lambda i, j: (i, j)),
        compiler_params=_params("parallel", "arbitrary"),
        name="gates",
    )(x, w_g)


def _sb_attn_kernel(q_ref, k_ref, vt_ref, o_ref, *, tq, tk, q_off, n_heads):
    i = pl.program_id(2)
    t0 = q_off + i * tq
    hd = SB_HEAD_DIM
    heads = range(n_heads)
    qs = [q_ref[:, h * hd:(h + 1) * hd] for h in heads]
    rel = lax.broadcasted_iota(jnp.int32, (tk, tq), 0) - lax.broadcasted_iota(jnp.int32, (tk, tq), 1)
    r = lax.broadcasted_iota(jnp.int32, (tk, tk), 0)
    c = lax.broadcasted_iota(jnp.int32, (tk, tk), 1)
    upper = (c >= r).astype(BF16)

    def cond(carry):
        j, runs, _ = carry
        lowest = functools.reduce(jnp.minimum, runs)
        return jnp.logical_and(j >= 0, jnp.min(lowest) < SB_SKIP)

    def body(carry):
        j, runs, accs = carry
        k0 = pl.multiple_of(j * tk, tk)
        mask = rel < (t0 - k0)
        new_runs, new_accs = [], []
        for h in heads:
            z = _dot_nt(k_ref[pl.ds(k0, tk), h * hd:(h + 1) * hd], qs[h])
            sp = jnp.maximum(z, 0.0) + jnp.log(1.0 + jnp.exp(-jnp.abs(z)))
            sp = jnp.where(mask, sp, 0.0)
            hi = sp.astype(BF16)
            lo = (sp - hi.astype(F32)).astype(BF16)
            cs = _dot(upper, hi) + _dot(upper, lo) + runs[h]
            a = jnp.where(mask, jnp.exp(z - cs), 0.0)
            new_accs.append(accs[h] + _dot(vt_ref[h, j], a.astype(BF16)))
            new_runs.append(cs[0:1, :])
        return j - 1, tuple(new_runs), tuple(new_accs)

    j_start = (t0 + tq - 2) // tk
    init = (j_start, tuple(jnp.zeros((1, tq), F32) for _ in heads),
            tuple(jnp.zeros((hd, tq), F32) for _ in heads))
    _, _, accs = lax.while_loop(cond, body, init)
    for h in heads:
        o_ref[:, h * hd:(h + 1) * hd] = accs[h].T.astype(o_ref.dtype)


def _sb_attn(q, k, vt, *, n_batch, tq_total, q_off, tq, tk, n_heads=4):
    tk_total = k.shape[0] // n_batch
    nq = tq_total // tq
    hw = n_heads * SB_HEAD_DIM
    return pl.pallas_call(
        functools.partial(_sb_attn_kernel, tq=tq, tk=tk, q_off=q_off, n_heads=n_heads),
        out_shape=jax.ShapeDtypeStruct((n_batch * tq_total, SB_HEADS * SB_HEAD_DIM), BF16),
        grid=(n_batch, SB_HEADS // n_heads, nq),
        in_specs=[pl.BlockSpec((tq, hw), lambda b, g, i: (b * nq + i, g)),
                  pl.BlockSpec((tk_total, hw), lambda b, g, i: (b, g)),
                  pl.BlockSpec((None, n_heads, tk_total // tk, SB_HEAD_DIM, tk),
                               lambda b, g, i: (b, g, 0, 0, 0))],
        out_specs=pl.BlockSpec((tq, hw), lambda b, g, i: (b * nq + i, g)),
        compiler_params=_params("parallel", "parallel", "arbitrary"),
        name="sb_attn",
    )(q, k, vt)


def _sb_decode_kernel(q_ref, kn_ref, vn_ref, kc_ref, vc_ref, o_ref, *, dec, tk):
    hd = SB_HEAD_DIM
    heads = range(SB_HEADS)
    rows = SB_HEADS * dec
    past = kc_ref.shape[0] // SB_HEADS

    def softplus(z):
        return jnp.maximum(z, 0.0) + jnp.log(1.0 + jnp.exp(-jnp.abs(z)))

    def lower(n):
        return (lax.broadcasted_iota(jnp.int32, (n, n), 0) >= lax.broadcasted_iota(jnp.int32, (n, n), 1)).astype(BF16)

    def reverse_cumsum(sp, tri):
        hi = sp.astype(BF16)
        lo = (sp - hi.astype(F32)).astype(BF16)
        return _dot(hi, tri) + _dot(lo, tri)

    def head_cols(x, h):
        return x[:, h * hd:(h + 1) * hd]

    qs = [head_cols(q_ref[...], h) for h in heads]

    kn = kn_ref[...].astype(BF16)
    vn = vn_ref[...].astype(BF16)
    z = jnp.concatenate([_dot_nt(qs[h], head_cols(kn, h)) for h in heads], axis=0)
    qt = lax.rem(lax.broadcasted_iota(jnp.int32, (rows, dec), 0), dec)
    mask = lax.broadcasted_iota(jnp.int32, (rows, dec), 1) < qt
    cs = reverse_cumsum(jnp.where(mask, softplus(z), 0.0), lower(dec))
    a = jnp.where(mask, jnp.exp(z - cs), 0.0).astype(BF16)
    acc0 = jnp.concatenate([_dot(a[h * dec:(h + 1) * dec, :], head_cols(vn, h)) for h in heads], axis=0)
    tri = lower(tk)

    def cond(carry):
        j, run, _ = carry
        return jnp.logical_and(j >= 0, jnp.min(run) < SB_SKIP)

    def body(carry):
        j, run, acc = carry
        r0 = pl.multiple_of(j * (tk * SB_HEADS), tk * SB_HEADS)

        def head_rows(ref, h):
            return ref[pl.ds(r0 + h, tk, stride=SB_HEADS), :].astype(BF16)

        z = jnp.concatenate([_dot_nt(qs[h], head_rows(kc_ref, h)) for h in heads], axis=0)
        cs = reverse_cumsum(softplus(z), tri) + run
        a = jnp.exp(z - cs).astype(BF16)
        pv = jnp.concatenate([_dot(a[h * dec:(h + 1) * dec, :], head_rows(vc_ref, h)) for h in heads], axis=0)
        return j - 1, cs[:, 0:1], acc + pv

    _, _, acc = lax.while_loop(cond, body, (past // tk - 1, cs[:, 0:1], acc0))
    for h in heads:
        o_ref[:, h * hd:(h + 1) * hd] = acc[h * dec:(h + 1) * dec, :].astype(o_ref.dtype)


def _sb_decode(q, k_new, v_new, cache_k, cache_v, *, dec, tk=256):
    nd, rows, hd = cache_k.shape
    past = rows // SB_HEADS
    assert past % tk == 0 and hd == SB_HEAD_DIM
    width = SB_HEADS * hd
    new = pl.BlockSpec((dec, width), lambda b: (b, 0))
    cache = pl.BlockSpec((None, rows, hd), lambda b: (b, 0, 0))
    return pl.pallas_call(
        functools.partial(_sb_decode_kernel, dec=dec, tk=tk),
        out_shape=jax.ShapeDtypeStruct((nd * dec, width), BF16),
        grid=(nd,),
        in_specs=[new, new, new, cache, cache],
        out_specs=new,
        compiler_params=_params("parallel"),
        name="sb_decode",
    )(q, k_new, v_new, cache_k, cache_v)


def _mla_attn_kernel(q_ref, kt_ref, v_ref, o_ref, s_scr, p_scr, acc_scr, m_scr, *, t):
    i = pl.program_id(2)
    ones = jnp.ones((t, V_HEAD_DIM), BF16)
    heads = range(q_ref.shape[0])

    def scores(h, j):
        return _dot(q_ref[h], kt_ref[h, j])

    def weighted_values(h, p, j):
        v_ext = jnp.concatenate([v_ref[h, pl.ds(pl.multiple_of(j * t, t), t), :], ones], axis=1)
        return _dot(p, v_ext)

    def step(j, a, diagonal=False):
        for h in heads:
            if not diagonal:
                s_scr[h, 1 - a] = scores(h, j + 1)
            pv = weighted_values(h, p_scr[h, 1 - a], jnp.maximum(j - 1, 0))
            s = s_scr[h, a]
            if diagonal:
                qchunk = lax.broadcasted_iota(jnp.int32, (t, t), 0) // CHUNK
                kchunk = lax.broadcasted_iota(jnp.int32, (t, t), 1) // CHUNK
                s = jnp.where(kchunk <= qchunk, s, NEG_INF)
            m = m_scr[h]
            m_new = jnp.maximum(m, jnp.max(s, axis=1, keepdims=True))
            p = jnp.exp2(s - m_new).astype(BF16)
            acc = jnp.exp2(m - m_new) * (acc_scr[h] + pv)
            if diagonal:
                acc = acc + weighted_values(h, p, j)
                o = acc[:, :V_HEAD_DIM] / acc[:, V_HEAD_DIM:]
                o_ref[:, h * V_HEAD_DIM:(h + 1) * V_HEAD_DIM] = o.astype(o_ref.dtype)
            else:
                m_scr[h] = m_new
                p_scr[h, a] = p
                acc_scr[h] = acc

    for h in heads:
        s_scr[h, 0] = scores(h, 0)
        p_scr[h, 1] = jnp.zeros((t, t), BF16)
    acc_scr[...] = jnp.zeros_like(acc_scr)
    m_scr[...] = jnp.full(m_scr.shape, NEG_INF, F32)

    def pair(jj, _):
        step(2 * jj, 0)
        step(2 * jj + 1, 1)
        return 0

    lax.fori_loop(0, i // 2, pair, 0)
    odd = lax.rem(i, 2) == 1

    @pl.when(odd)
    def _():
        step(i - 1, 0)
        step(i, 1, diagonal=True)

    @pl.when(jnp.logical_not(odd))
    def _():
        step(i, 0, diagonal=True)


def _mla_attn(q_cat, kt, v, *, n_batch, seq, hb=2):
    t = kt.shape[3]
    assert seq % t == 0 and t % CHUNK == 0 and MLA_HEADS % hb == 0
    nq = seq // t
    return pl.pallas_call(
        functools.partial(_mla_attn_kernel, t=t),
        out_shape=jax.ShapeDtypeStruct((n_batch * seq, MLA_HEADS * V_HEAD_DIM), BF16),
        grid=(n_batch, MLA_HEADS // hb, nq),
        in_specs=[pl.BlockSpec((hb, t, QK_DIM), lambda b, g, i: (g, b * nq + i, 0)),
                  pl.BlockSpec((hb, nq, QK_DIM, t), lambda b, g, i: (g, b, 0, 0)),
                  pl.BlockSpec((hb, seq, V_HEAD_DIM), lambda b, g, i: (g, b, 0))],
        out_specs=pl.BlockSpec((t, hb * V_HEAD_DIM), lambda b, g, i: (b * nq + i, g)),
        scratch_shapes=[pltpu.VMEM((hb, 2, t, t), F32), pltpu.VMEM((hb, 2, t, t), BF16),
                        pltpu.VMEM((hb, t, 2 * V_HEAD_DIM), F32), pltpu.VMEM((hb, t, 1), F32)],
        compiler_params=_params("parallel", "parallel", "arbitrary"),
        name="mla_attn",
    )(q_cat, kt, v)


def _mla_decode_kernel(q_ref, wuk_ref, wuv_ref, cckv_ref, ckr_ref, nckv_ref, nkr_ref, o_ref, *, past, dec):
    rows = MLA_HEADS * dec
    ql = jnp.concatenate([_dot(q_ref[h, :, 0:QK_NOPE_DIM], wuk_ref[h]) for h in range(MLA_HEADS)], axis=0)
    ql = ql.astype(BF16)
    qr = jnp.concatenate([q_ref[h, :, QK_NOPE_DIM:QK_DIM] for h in range(MLA_HEADS)], axis=0)
    kc = cckv_ref[...].astype(BF16)
    kn = nckv_ref[...].astype(BF16)
    s_c = _dot_nt(ql, kc) + _dot_nt(qr, ckr_ref[...].astype(BF16))
    s_n = _dot_nt(ql, kn) + _dot_nt(qr, nkr_ref[...].astype(BF16))
    qpos = past + lax.rem(lax.broadcasted_iota(jnp.int32, (rows, dec), 0), dec)
    kpos = past + lax.broadcasted_iota(jnp.int32, (rows, dec), 1)
    s_n = jnp.where(kpos // CHUNK <= qpos // CHUNK, s_n, NEG_INF)
    m = jnp.maximum(jnp.max(s_c, axis=1, keepdims=True), jnp.max(s_n, axis=1, keepdims=True))
    p_c = jnp.exp2(s_c - m)
    p_n = jnp.exp2(s_n - m)
    l = jnp.sum(p_c, axis=1, keepdims=True) + jnp.sum(p_n, axis=1, keepdims=True)
    ol = (_dot(p_c.astype(BF16), kc) + _dot(p_n.astype(BF16), kn)) / l
    ol = ol.astype(BF16)
    for h in range(MLA_HEADS):
        o_h = _dot(ol[h * dec:(h + 1) * dec, :], wuv_ref[h])
        o_ref[:, h * V_HEAD_DIM:(h + 1) * V_HEAD_DIM] = o_h.astype(o_ref.dtype)


def _mla_decode(q_cat, w_uk, w_uv, cache_ckv, cache_kr, ckv, kr, *, q_blk0, dec):
    nd, past, kv_lora = cache_ckv.shape
    const3 = lambda b: (0, 0, 0)
    return pl.pallas_call(
        functools.partial(_mla_decode_kernel, past=past, dec=dec),
        out_shape=jax.ShapeDtypeStruct((nd * dec, MLA_HEADS * V_HEAD_DIM), BF16),
        grid=(nd,),
        in_specs=[pl.BlockSpec((MLA_HEADS, dec, QK_DIM), lambda b: (0, q_blk0 + b, 0)),
                  pl.BlockSpec(w_uk.shape, const3),
                  pl.BlockSpec(w_uv.shape, const3),
                  pl.BlockSpec((None, past, kv_lora), lambda b: (b, 0, 0)),
                  pl.BlockSpec((None, past, ROPE_DIM), lambda b: (b, 0, 0)),
                  pl.BlockSpec((dec, kv_lora), lambda b: (q_blk0 + b, 0)),
                  pl.BlockSpec((dec, ROPE_DIM), lambda b: (q_blk0 + b, 0))],
        out_specs=pl.BlockSpec((dec, MLA_HEADS * V_HEAD_DIM), lambda b: (b, 0)),
        compiler_params=_params("parallel"),
        name="mla_decode",
    )(q_cat, w_uk, w_uv, cache_ckv, cache_kr, ckv, kr)


def _merge_kernel(x_ref, osb_ref, omla_ref, gates_ref, wa_ref, wb_ref, wo_ref, g_ref, b_ref, *rest,
                  alpha, n_own):
    o_ref, u_ref = rest[-2:]
    i = pl.program_id(0)
    d = o_ref.shape[1]

    @pl.when(i < n_own)
    def _():
        osb = osb_ref[...]
        omla = omla_ref[...]
        for c in range(0, d, COL_CHUNK):
            ga = gates_ref[:, c:c + COL_CHUNK].astype(F32)
            gb = gates_ref[:, d + c:d + c + COL_CHUNK].astype(F32)
            u = ga * _dot(osb, wa_ref[:, c:c + COL_CHUNK]) + gb * _dot(omla, wb_ref[:, c:c + COL_CHUNK])
            u_ref[:, c:c + COL_CHUNK] = u.astype(BF16)
        mix = _dot(u_ref[...], wo_ref[...])
        o_ref[...] = _layer_norm(alpha * x_ref[...] + mix, g_ref[...], b_ref[...])

    @pl.when(i >= n_own)
    def _():
        o_ref[...] = jnp.zeros_like(o_ref)


def _merge(x, o_sb, o_mla, gates, w_br_a, w_br_b, w_o, ln_g, ln_b, *, alpha, out_rows, row0=0, into=None,
           tm=ROW_TILE):
    m, d = x.shape
    blk0 = row0 // tm
    n_own = m // tm
    n_rows_tiles = n_own if into is not None else out_rows // tm
    own = lambda i: (jnp.minimum(i, n_own - 1), 0)
    const = lambda i: (0, 0)
    resident = lambda w: pl.BlockSpec(w.shape, const, pipeline_mode=pl.Buffered(1))
    extra_specs = [] if into is None else [pl.BlockSpec(memory_space=pl.ANY)]
    extra_args = [] if into is None else [into]
    return pl.pallas_call(
        functools.partial(_merge_kernel, alpha=alpha, n_own=n_own),
        out_shape=jax.ShapeDtypeStruct((out_rows, d), F32),
        grid=(n_rows_tiles,),
        input_output_aliases={} if into is None else {9: 0},
        in_specs=[pl.BlockSpec((tm, d), own),
                  pl.BlockSpec((tm, o_sb.shape[1]), own),
                  pl.BlockSpec((tm, o_mla.shape[1]), own),
                  pl.BlockSpec((tm, 2 * d), own),
                  resident(w_br_a), resident(w_br_b), resident(w_o),
                  pl.BlockSpec((1, d), const),
                  pl.BlockSpec((1, d), const)] + extra_specs,
        out_specs=pl.BlockSpec((tm, d), lambda i: (blk0 + i, 0)),
        scratch_shapes=[pltpu.VMEM((tm, d), BF16)],
        compiler_params=_params("parallel"),
        name="merge_ln1",
    )(x, o_sb, o_mla, gates, w_br_a, w_br_b, w_o, ln_g, ln_b, *extra_args)


def _router_kernel(x_ref, wh_ref, wl_ref, o_ref):
    x = x_ref[...]
    xh = x.astype(BF16)
    xl = (x - xh.astype(F32)).astype(BF16)
    logits = _dot(xh, wh_ref[...]) + (_dot(xh, wl_ref[...]) + _dot(xl, wh_ref[...]))
    lane = lax.broadcasted_iota(jnp.int32, logits.shape, 1)
    big = jnp.int32(LANES)
    is_grp = lane < N_GROUPS
    gl = jnp.where(is_grp, logits, -jnp.inf)
    gmax = jnp.max(gl, axis=1, keepdims=True)
    gsum = jnp.sum(jnp.where(is_grp, jnp.exp(gl - gmax), 0.0), axis=1, keepdims=True)
    p_grp = 1.0 / gsum
    grp = jnp.min(jnp.where(jnp.logical_and(is_grp, gl == gmax), lane, big), axis=1, keepdims=True)
    lo = N_GROUPS + grp * EXPERTS_PER_GROUP
    in_grp = jnp.logical_and(lane >= lo, lane < lo + EXPERTS_PER_GROUP)
    el = jnp.where(in_grp, logits, -jnp.inf)
    v1 = jnp.max(el, axis=1, keepdims=True)
    i1 = jnp.min(jnp.where(el == v1, lane, big), axis=1, keepdims=True)
    el2 = jnp.where(lane == i1, -jnp.inf, el)
    v2 = jnp.max(el2, axis=1, keepdims=True)
    i2 = jnp.min(jnp.where(el2 == v2, lane, big), axis=1, keepdims=True)
    e2w = jnp.exp(v2 - v1)
    w1 = p_grp / (1.0 + e2w)
    w2 = p_grp * e2w / (1.0 + e2w)
    out = jnp.where(lane == 0, (i1 - N_GROUPS).astype(F32),
                    jnp.where(lane == 1, (i2 - N_GROUPS).astype(F32),
                              jnp.where(lane == 2, w1, jnp.where(lane == 3, w2, 0.0))))
    o_ref[...] = out


def _router(x1, w_hi, w_lo, tm=ROW_TILE):
    m, d = x1.shape
    return pl.pallas_call(
        _router_kernel,
        out_shape=jax.ShapeDtypeStruct((m, LANES), F32),
        grid=(m // tm,),
        in_specs=[pl.BlockSpec((tm, d), lambda i: (i, 0)),
                  pl.BlockSpec((d, LANES), lambda i: (0, 0)),
                  pl.BlockSpec((d, LANES), lambda i: (0, 0))],
        out_specs=pl.BlockSpec((tm, LANES), lambda i: (i, 0)),
        compiler_params=_params("parallel"),
        name="router",
    )(x1, w_hi, w_lo)


def _row_gather_start(idx_ref, src_hbm, dst_ref, sem, n_rows, unrolled=False, priorities=(0,)):
    def start(r, priority):
        tok = idx_ref[0, 0, r]
        pltpu.make_async_copy(src_hbm.at[pl.ds(tok, 1)], dst_ref.at[pl.ds(r, 1)], sem).start(priority=priority)

    if unrolled:
        for r in range(n_rows):
            start(r, priorities[r % len(priorities)])
    else:
        def body(r, _):
            start(r, 0)
            return 0
        lax.fori_loop(0, n_rows, body, 0)


def _row_gather_wait(src_hbm, dst_ref, sem, n_rows):
    pltpu.make_async_copy(src_hbm.at[pl.ds(0, n_rows)], dst_ref, sem).wait()


GATHER_RING = 3
COL_CHUNK = 512


def _moe_kernel(te_ref, nu_ref, ws_ref, nx_ref, idx_ref, idx1_ref, idx2_ref, x_hbm, wg_hbm, wu_hbm, wd_hbm,
                rw_ref, y_ref, xbuf0, xbuf1, xbuf2, sem, wg_f, wu_f, wd_f, wsem, wgb, wub, wdb, *, tm):
    t = pl.program_id(0)
    n_used = nu_ref[0]
    bufs = (xbuf0, xbuf1, xbuf2)

    def weight_copies(e, slot):
        return [pltpu.make_async_copy(w_hbm.at[e], w_f.at[slot], wsem.at[slot])
                for w_hbm, w_f in ((wg_hbm, wg_f), (wu_hbm, wu_f), (wd_hbm, wd_f))]

    @pl.when(t == 0)
    def _():
        _row_gather_start(idx_ref, x_hbm, xbuf0, sem.at[0], tm)
        _row_gather_start(idx1_ref, x_hbm, xbuf1, sem.at[1], tm)
        for cp in weight_copies(te_ref[0], 0):
            cp.start(priority=1)

    first_of_expert = jnp.logical_or(t == 0, te_ref[t] != te_ref[jnp.maximum(t - 1, 0)])

    @pl.when(jnp.logical_and(first_of_expert, t < n_used))
    def _():
        slot = ws_ref[t]
        for cp in weight_copies(te_ref[t], slot):
            cp.wait()

        @pl.when(nx_ref[t] >= 0)
        def _():
            for cp in weight_copies(nx_ref[t], 1 - slot):
                cp.start(priority=1)

        wgb[...] = wg_f[slot].astype(BF16)
        wub[...] = wu_f[slot].astype(BF16)
        wdb[...] = wd_f[slot].astype(BF16)

    def run(a):
        ahead = (a + 2) % GATHER_RING

        @pl.when(t <= n_used + 1)
        def _():
            _row_gather_wait(x_hbm, bufs[a], sem.at[a], tm)

        @pl.when(t < n_used)
        def _():
            _row_gather_start(idx2_ref, x_hbm, bufs[ahead], sem.at[ahead], tm, unrolled=True)
            xb = bufs[a][...].astype(BF16)
            g = _dot(xb, wgb[...])
            u = _dot(xb, wub[...])
            h = (g / (1.0 + jnp.exp(-g))) * u
            y_ref[...] = _dot(h.astype(BF16), wdb[...]) * rw_ref[:, 0:1]

    slot = lax.rem(t, GATHER_RING)
    for a in range(GATHER_RING):
        pl.when(slot == a)(functools.partial(run, a))

    @pl.when(t >= n_used)
    def _():
        y_ref[...] = jnp.zeros_like(y_ref)


def _moe(tile_expert, n_used, w_slot, next_expert, row_token, x1, w_gate, w_up, w_down, row_w, *, tm):
    n_tiles = tile_expert.shape[0]
    d = x1.shape[1]
    f = w_gate.shape[2]
    idx3 = row_token.reshape(n_tiles, 1, tm)
    ahead = lambda k: pl.BlockSpec((1, 1, tm), lambda t, *_: (jnp.minimum(t + k, n_tiles - 1), 0, 0),
                                   memory_space=pltpu.SMEM)
    hbm = pl.BlockSpec(memory_space=pl.ANY)
    grid_spec = pltpu.PrefetchScalarGridSpec(
        num_scalar_prefetch=4,
        grid=(n_tiles,),
        in_specs=[ahead(0), ahead(1), ahead(2), hbm, hbm, hbm, hbm,
                  pl.BlockSpec((tm, LANES), lambda t, *_: (t, 0))],
        out_specs=pl.BlockSpec((tm, d), lambda t, *_: (t, 0)),
        scratch_shapes=[pltpu.VMEM((tm, d), F32) for _ in range(GATHER_RING)] + [
                        pltpu.SemaphoreType.DMA((GATHER_RING,)),
                        pltpu.VMEM((2, d, f), F32),
                        pltpu.VMEM((2, d, f), F32),
                        pltpu.VMEM((2, f, d), F32),
                        pltpu.SemaphoreType.DMA((2,)),
                        pltpu.VMEM((d, f), BF16),
                        pltpu.VMEM((d, f), BF16),
                        pltpu.VMEM((f, d), BF16)],
    )
    return pl.pallas_call(
        functools.partial(_moe_kernel, tm=tm),
        out_shape=jax.ShapeDtypeStruct((n_tiles * tm, d), F32),
        grid_spec=grid_spec,
        compiler_params=_params("arbitrary"),
        name="moe_experts",
    )(tile_expert, n_used, w_slot, next_expert, idx3, idx3, idx3, x1, w_gate, w_up, w_down, row_w)


def _final_kernel(p1_ref, p2_ref, p1a_ref, p2a_ref, p1b_ref, p2b_ref, y_hbm, x1_ref, pe_ref, wpp_ref, wpg_ref,
                  g_ref, b_ref, o_ref, ya0, yb0, ya1, yb1, ya2, yb2, sem, *, tm, alpha):
    t = pl.program_id(0)
    last = pl.num_programs(0) - 1
    bufs = ((ya0, yb0), (ya1, yb1), (ya2, yb2))

    def start(pa_ref, pb_ref, s, unrolled):
        _row_gather_start(pa_ref, y_hbm, bufs[s][0], sem.at[s], tm, unrolled=unrolled, priorities=(0, 1))
        _row_gather_start(pb_ref, y_hbm, bufs[s][1], sem.at[s], tm, unrolled=unrolled, priorities=(0, 1))

    def wait(s):
        _row_gather_wait(y_hbm, bufs[s][0], sem.at[s], tm)
        _row_gather_wait(y_hbm, bufs[s][1], sem.at[s], tm)

    @pl.when(t == 0)
    def _():
        start(p1_ref, p2_ref, 0, False)
        start(p1a_ref, p2a_ref, 1, False)

    def run(a):
        wait(a)
        start(p1b_ref, p2b_ref, (a + 2) % GATHER_RING, True)
        f = bufs[a][0][...] + bufs[a][1][...]
        x2 = _layer_norm(alpha * x1_ref[...] + f, g_ref[...], b_ref[...])
        o_ref[...] = x2
        x2b = x2.astype(BF16)
        peb = pe_ref[...].astype(BF16)
        n = o_ref.shape[1]
        for c in range(0, n, COL_CHUNK):
            gate = 1.0 / (1.0 + jnp.exp(-_dot(x2b, wpg_ref[:, c:c + COL_CHUNK])))
            o_ref[:, c:c + COL_CHUNK] += gate * _dot(peb, wpp_ref[:, c:c + COL_CHUNK])

        @pl.when(t == last)
        def _():
            wait((a + 1) % GATHER_RING)
            wait((a + 2) % GATHER_RING)

    slot = lax.rem(t, GATHER_RING)
    for a in range(GATHER_RING):
        pl.when(slot == a)(functools.partial(run, a))


def _final(pos1, pos2, y_sorted, x1, x1_blk0, pe, w_pp, w_pg, ln_g, ln_b, *, alpha, tm=ROW_TILE):
    d = x1.shape[1]
    m = pe.shape[0]
    n_tiles = m // tm
    p1 = pos1.reshape(n_tiles, 1, tm)
    p2 = pos2.reshape(n_tiles, 1, tm)
    ahead = lambda k: pl.BlockSpec((1, 1, tm), lambda t: (jnp.minimum(t + k, n_tiles - 1), 0, 0),
                                   memory_space=pltpu.SMEM)
    const = lambda t: (0, 0)
    return pl.pallas_call(
        functools.partial(_final_kernel, tm=tm, alpha=alpha),
        out_shape=jax.ShapeDtypeStruct((m, d), F32),
        grid=(n_tiles,),
        in_specs=[ahead(0), ahead(0), ahead(1), ahead(1), ahead(2), ahead(2),
                  pl.BlockSpec(memory_space=pl.ANY),
                  pl.BlockSpec((tm, d), lambda t: (x1_blk0 + t, 0)),
                  pl.BlockSpec((tm, pe.shape[1]), lambda t: (t, 0)),
                  pl.BlockSpec(w_pp.shape, const),
                  pl.BlockSpec(w_pg.shape, const),
                  pl.BlockSpec((1, d), const),
                  pl.BlockSpec((1, d), const)],
        out_specs=pl.BlockSpec((tm, d), lambda t: (t, 0)),
        scratch_shapes=[pltpu.VMEM((tm, d), F32) for _ in range(2 * GATHER_RING)] + [
                        pltpu.SemaphoreType.DMA((GATHER_RING,))],
        compiler_params=_params("arbitrary"),
        name="combine_ln2_ple",
    )(p1, p2, p1, p2, p1, p2, y_sorted, x1, pe, w_pp, w_pg, ln_g, ln_b)


def _rotate_half_cols(w):
    half = ROPE_DIM // 2
    return jnp.concatenate([-w[..., half:], w[..., :half]], axis=-1)


def _rope_tables(pos):
    inv_freq = 1.0 / (ROPE_THETA ** (jnp.arange(0, ROPE_DIM, 2, dtype=F32) / ROPE_DIM))
    ang = pos.astype(F32)[:, None] * inv_freq[None, :]
    ang = jnp.concatenate([ang, ang, ang, ang], axis=-1)
    return jnp.cos(ang), jnp.sin(ang)


def _route_tables(route, tm):
    m = route.shape[0]
    e_pair = jnp.concatenate([route[:, 0], route[:, 1]]).astype(jnp.int32)
    w_pair = jnp.concatenate([route[:, 2], route[:, 3]])
    tok = jnp.arange(m, dtype=jnp.int32)
    tok_pair = jnp.concatenate([tok, tok])
    onehot = (e_pair[:, None] == jnp.arange(N_EXPERTS, dtype=jnp.int32)[None, :]).astype(jnp.int32)
    csum = jnp.cumsum(onehot, axis=0)
    rank = jnp.sum(csum * onehot, axis=1) - 1
    counts = csum[-1]
    padded = ((counts + tm - 1) // tm) * tm
    pad_end = jnp.cumsum(padded)
    pad_off = pad_end - padded
    pos = pad_off[e_pair] + rank
    n_rows = (2 * m // tm + N_EXPERTS - 1 + GATHER_RING - 1) * tm
    payload = jnp.stack([tok_pair, lax.bitcast_convert_type(w_pair, jnp.int32)], axis=1)
    rows = jnp.zeros((n_rows, 2), jnp.int32).at[pos].set(payload)
    row_token = rows[:, 0]
    row_w = lax.bitcast_convert_type(rows[:, 1], F32)
    tile_start = jnp.arange(n_rows // tm, dtype=jnp.int32) * tm
    tile_expert = jnp.sum((pad_end[None, :] <= tile_start[:, None]).astype(jnp.int32), axis=1)
    tile_expert = jnp.minimum(tile_expert, N_EXPERTS - 1)
    n_used = (pad_end[-1] // tm).astype(jnp.int32).reshape(1)
    last_expert = tile_expert[jnp.maximum(n_used[0] - 1, 0)]
    tile_expert = jnp.where(tile_start // tm < n_used[0], tile_expert, last_expert).astype(jnp.int32)
    changed = jnp.concatenate([jnp.zeros((1,), jnp.int32),
                               (tile_expert[1:] != tile_expert[:-1]).astype(jnp.int32)])
    w_slot = (jnp.cumsum(changed) % 2).astype(jnp.int32)
    experts = jnp.arange(N_EXPERTS, dtype=jnp.int32)
    later_nonempty = jnp.logical_and(experts[None, :] > experts[:, None], counts[None, :] > 0)
    next_nonempty = jnp.min(jnp.where(later_nonempty, experts[None, :], N_EXPERTS), axis=1)
    next_nonempty = jnp.where(next_nonempty == N_EXPERTS, -1, next_nonempty).astype(jnp.int32)
    next_expert = next_nonempty[tile_expert]
    row_w = jnp.broadcast_to(row_w[:, None], (n_rows, LANES))
    return tile_expert, n_used, w_slot, next_expert, row_token, row_w, pos[:m], pos[m:]


def kernel(x_prompt, x_sample, cache_sb_k, cache_sb_v, cache_mla_ckv, cache_mla_krope, p_prompt, p_sample,
           w_in, q_a_norm, kv_a_norm, w_uq, w_ukv, w_br_a, w_br_b, w_o, ln1_g, ln1_b, ln2_g, ln2_b,
           w_router_group, w_router_expert, w_exp_gate, w_exp_up, w_exp_down, w_ple_proj, w_ple_gate):
    depth = w_in.shape[0]
    assert depth == 1, "single trunk layer"
    nb, seq, d = x_prompt.shape
    nd, dec, _ = x_sample.shape
    past = cache_sb_k.shape[2]
    sbw = SB_HEADS * SB_HEAD_DIM
    q_lora = q_a_norm.shape[1]
    kv_lora = kv_a_norm.shape[1]
    alpha = (2.0 * depth) ** 0.25
    mp = nb * seq
    ms = nd * dec

    w0 = w_in[0]
    o1 = 3 * sbw
    o2 = o1 + q_lora
    o3 = o2 + kv_lora
    o4 = o3 + ROPE_DIM
    w_sb = w0[:, :o1].astype(BF16)
    w_kr = w0[:, o3:o4]
    w_lat = jnp.concatenate([w0[:, o1:o3], w_kr, _rotate_half_cols(w_kr)], axis=1).astype(BF16)
    w_g = w0[:, o4:].astype(BF16)
    wq = w_uq[0].reshape(q_lora, MLA_HEADS, QK_DIM)
    wq_r = wq[:, :, QK_NOPE_DIM:]
    w_uq_all = jnp.concatenate([wq[:, :, :QK_NOPE_DIM].reshape(q_lora, -1), wq_r.reshape(q_lora, -1),
                                _rotate_half_cols(wq_r).reshape(q_lora, -1)], axis=1).astype(BF16)
    wkv = w_ukv[0].reshape(kv_lora, MLA_HEADS, QK_NOPE_DIM + V_HEAD_DIM)
    w_ukv_all = jnp.concatenate([wkv[:, :, :QK_NOPE_DIM].reshape(kv_lora, -1),
                                 wkv[:, :, QK_NOPE_DIM:].reshape(kv_lora, -1)], axis=1).astype(BF16)
    w_r = jnp.concatenate([w_router_group[0], w_router_expert[0]], axis=1)
    w_r = jnp.pad(w_r, ((0, 0), (0, LANES - w_r.shape[1])))
    w_r_hi = w_r.astype(BF16)
    w_r_lo = (w_r - w_r_hi.astype(F32)).astype(BF16)

    w_uk = jnp.transpose(wkv[:, :, :QK_NOPE_DIM], (1, 2, 0)).astype(BF16)
    w_uv = jnp.transpose(wkv[:, :, QK_NOPE_DIM:], (1, 0, 2)).astype(BF16)
    w_a, w_b, w_out = w_br_a[0].astype(BF16), w_br_b[0].astype(BF16), w_o[0].astype(BF16)
    w_pp, w_pg = w_ple_proj[0].astype(BF16), w_ple_gate[0].astype(BF16)

    xp = x_prompt.reshape(mp, d)
    cos_p, sin_p = _rope_tables(jnp.arange(seq, dtype=jnp.int32))
    tile = 256
    sbq_p, sbk_p, sbv_p, sbk_b, vt_p = _sb_proj(xp, w_sb, seq=seq, tm=tile)
    q_cat_p, ckv_p, kr_p = _mla_q_proj(xp, w_lat, q_a_norm, kv_a_norm, w_uq_all, cos_p, sin_p)
    gates_p = _gates(xp, w_g)
    o_sb_p = _sb_attn(sbq_p, sbk_b, vt_p, n_batch=nb, tq_total=seq, q_off=0, tq=tile, tk=tile)
    kt_mla, v_mla = _mla_kv_proj(ckv_p, kr_p, w_ukv_all, tm=512)
    o_mla_p = _mla_attn(q_cat_p, kt_mla, v_mla, n_batch=nb, seq=seq)

    xs = x_sample.reshape(ms, d)
    cos_s, sin_s = _rope_tables(jnp.tile(past + jnp.arange(dec, dtype=jnp.int32), ROW_TILE // dec))
    sbq_s, sbk_s, sbv_s = _sb_proj(xs, w_sb)
    q_cat_s, ckv_s, kr_s = _mla_q_proj(xs, w_lat, q_a_norm, kv_a_norm, w_uq_all, cos_s, sin_s)
    gates_s = _gates(xs, w_g)
    o_sb_s = _sb_decode(sbq_s, sbk_s, sbv_s, cache_sb_k.reshape(nd, past * SB_HEADS, SB_HEAD_DIM),
                        cache_sb_v.reshape(nd, past * SB_HEADS, SB_HEAD_DIM), dec=dec)
    o_mla_s = _mla_decode(q_cat_s, w_uk, w_uv, cache_mla_ckv[0], cache_mla_krope[0], ckv_s, kr_s,
                          q_blk0=0, dec=dec)

    x1 = _merge(xp, o_sb_p, o_mla_p, gates_p, w_a, w_b, w_out, ln1_g, ln1_b, alpha=alpha, out_rows=mp + ms)
    x1 = _merge(xs, o_sb_s, o_mla_s, gates_s, w_a, w_b, w_out, ln1_g, ln1_b, alpha=alpha, out_rows=mp + ms,
                row0=mp, into=x1)
    route = _router(x1, w_r_hi, w_r_lo)
    tile_expert, n_used, w_slot, next_expert, row_token, row_w, pos1, pos2 = _route_tables(route, ROW_TILE)
    y_sorted = _moe(tile_expert, n_used, w_slot, next_expert, row_token, x1, w_exp_gate[0], w_exp_up[0],
                    w_exp_down[0], row_w, tm=ROW_TILE)
    y_p = _final(pos1[:mp], pos2[:mp], y_sorted, x1, 0, p_prompt[0].reshape(mp, -1), w_pp, w_pg,
                 ln2_g, ln2_b, alpha=alpha)
    y_s = _final(pos1[mp:], pos2[mp:], y_sorted, x1, mp // ROW_TILE, p_sample[0].reshape(ms, -1), w_pp, w_pg,
                 ln2_g, ln2_b, alpha=alpha)

    hs = (SB_HEADS, SB_HEAD_DIM)
    return (y_p.reshape(nb, seq, d), y_s.reshape(nd, dec, d),
            sbk_p.reshape((1, nb, seq) + hs), sbv_p.reshape((1, nb, seq) + hs),
            ckv_p.reshape(1, nb, seq, kv_lora), kr_p.reshape(1, nb, seq, ROPE_DIM),
            sbk_s.reshape((1, nd, dec) + hs), sbv_s.reshape((1, nd, dec) + hs),
            ckv_s.reshape(1, nd, dec, kv_lora), kr_s.reshape(1, nd, dec, ROPE_DIM))
```

```python
import functools

import jax
import jax.numpy as jnp
from jax import lax
from jax.experimental import pallas as pl
from jax.experimental.pallas import tpu as pltpu

F32 = jnp.float32
BF16 = jnp.bfloat16

SB_HEADS = 8
SB_HEAD_DIM = 128
MLA_HEADS = 16
QK_NOPE_DIM = 128
ROPE_DIM = 64
V_HEAD_DIM = 128
QK_DIM = QK_NOPE_DIM + ROPE_DIM
CHUNK = 64
N_GROUPS = 4
EXPERTS_PER_GROUP = 8
N_EXPERTS = N_GROUPS * EXPERTS_PER_GROUP
ROPE_THETA = 10000.0
LN_EPS = 1e-5
RMS_EPS = 1e-6
NEG_INF = -1e30
MLA_SCALE = QK_DIM ** -0.5 * 1.4426950408889634
SB_SCALE = SB_HEAD_DIM ** -0.5
SB_SKIP = 120.0

VMEM_LIMIT = 56 * 1024 * 1024
ROW_TILE = 256
LANES = 128


def _params(*sem):
    return pltpu.CompilerParams(dimension_semantics=sem, vmem_limit_bytes=VMEM_LIMIT)


def _dot(a, b):
    return jnp.dot(a, b, preferred_element_type=F32)


def _dot_nt(a, b):
    return lax.dot_general(a, b, (((1,), (1,)), ((), ())), preferred_element_type=F32)


def _rms(x, g):
    ms = jnp.mean(x * x, axis=-1, keepdims=True)
    return x * lax.rsqrt(ms + RMS_EPS) * g


def _layer_norm(x, g, b):
    mu = jnp.mean(x, axis=-1, keepdims=True)
    xc = x - mu
    var = jnp.mean(xc * xc, axis=-1, keepdims=True)
    return xc * lax.rsqrt(var + LN_EPS) * g + b


def _sb_proj_kernel(x_ref, w_ref, q_ref, k_ref, v_ref, *tile_refs, width):
    xb = x_ref[...].astype(BF16)
    q = _dot(xb, w_ref[:, 0:width])
    q_ref[...] = (q * SB_SCALE).astype(BF16)
    k = _dot(xb, w_ref[:, width:2 * width])
    k_ref[...] = k
    v = _dot(xb, w_ref[:, 2 * width:3 * width])
    v_ref[...] = v
    if tile_refs:
        kb_ref, vt_ref = tile_refs
        kb_ref[...] = k.astype(BF16)
        for h in range(SB_HEADS):
            vt_ref[h] = v[:, h * SB_HEAD_DIM:(h + 1) * SB_HEAD_DIM].T.astype(BF16)


def _sb_proj(x, w_sb, *, seq=None, tm=ROW_TILE):
    m, d = x.shape
    width = w_sb.shape[1] // 3
    row = lambda i: (i, 0)
    out_sds = lambda dt: jax.ShapeDtypeStruct((m, width), dt)
    out_shape = [out_sds(BF16), out_sds(F32), out_sds(F32)]
    out_specs = [pl.BlockSpec((tm, width), row) for _ in range(3)]
    if seq is not None:
        nt = seq // tm
        out_shape += [out_sds(BF16), jax.ShapeDtypeStruct((m // seq, SB_HEADS, nt, SB_HEAD_DIM, tm), BF16)]
        out_specs += [pl.BlockSpec((tm, width), row),
                      pl.BlockSpec((None, SB_HEADS, None, SB_HEAD_DIM, tm), lambda i: (i // nt, 0, i % nt, 0, 0))]
    return pl.pallas_call(
        functools.partial(_sb_proj_kernel, width=width),
        out_shape=tuple(out_shape),
        grid=(m // tm,),
        in_specs=[pl.BlockSpec((tm, d), row), pl.BlockSpec((d, 3 * width), lambda i: (0, 0))],
        out_specs=tuple(out_specs),
        compiler_params=_params("parallel"),
        name="sb_proj",
    )(x, w_sb)


def _mla_q_proj_kernel(x_ref, wlat_ref, gq_ref, gkv_ref, wuq_ref, cos_ref, sin_ref,
                       q_ref, ckv_ref, kr_ref, *, q_lora, kv_lora):
    xb = x_ref[...].astype(BF16)
    cos = cos_ref[...]
    sin = sin_ref[...]
    ckv = _dot(xb, wlat_ref[:, q_lora:q_lora + kv_lora])
    ckv_ref[...] = _rms(ckv, gkv_ref[...])
    kr2 = _dot(xb, wlat_ref[:, q_lora + kv_lora:])
    kr_ref[...] = kr2[:, :ROPE_DIM] * cos[:, :ROPE_DIM] + kr2[:, ROPE_DIM:] * sin[:, :ROPE_DIM]
    cq = _dot(xb, wlat_ref[:, 0:q_lora])
    cqn = _rms(cq, gq_ref[...]).astype(BF16)
    n_nope = MLA_HEADS * QK_NOPE_DIM
    n_rope = MLA_HEADS * ROPE_DIM
    for h in range(MLA_HEADS):
        qn = _dot(cqn, wuq_ref[:, h * QK_NOPE_DIM:(h + 1) * QK_NOPE_DIM])
        q_ref[h, :, 0:QK_NOPE_DIM] = (qn * MLA_SCALE).astype(BF16)
    for hp in range(MLA_HEADS // 2):
        lo = n_nope + hp * LANES
        qr = _dot(cqn, wuq_ref[:, lo:lo + LANES])
        qrr = _dot(cqn, wuq_ref[:, lo + n_rope:lo + n_rope + LANES])
        rot = ((qr * cos + qrr * sin) * MLA_SCALE).astype(BF16)
        q_ref[2 * hp, :, QK_NOPE_DIM:QK_DIM] = rot[:, :ROPE_DIM]
        q_ref[2 * hp + 1, :, QK_NOPE_DIM:QK_DIM] = rot[:, ROPE_DIM:]


def _mla_q_proj(x, w_lat, gq, gkv, w_uq_all, cos, sin, tm=ROW_TILE):
    m, d = x.shape
    q_lora, kv_lora = gq.shape[1], gkv.shape[1]
    n_pos_tiles = cos.shape[0] // tm
    row = lambda i: (i, 0)
    pos = lambda i: (i % n_pos_tiles, 0)
    const = lambda i: (0, 0)
    return pl.pallas_call(
        functools.partial(_mla_q_proj_kernel, q_lora=q_lora, kv_lora=kv_lora),
        out_shape=(jax.ShapeDtypeStruct((MLA_HEADS, m, QK_DIM), BF16),
                   jax.ShapeDtypeStruct((m, kv_lora), F32),
                   jax.ShapeDtypeStruct((m, ROPE_DIM), F32)),
        grid=(m // tm,),
        in_specs=[pl.BlockSpec((tm, d), row),
                  pl.BlockSpec(w_lat.shape, const),
                  pl.BlockSpec((1, q_lora), const),
                  pl.BlockSpec((1, kv_lora), const),
                  pl.BlockSpec(w_uq_all.shape, const),
                  pl.BlockSpec((tm, LANES), pos),
                  pl.BlockSpec((tm, LANES), pos)],
        out_specs=(pl.BlockSpec((MLA_HEADS, tm, QK_DIM), lambda i: (0, i, 0)),
                   pl.BlockSpec((tm, kv_lora), row),
                   pl.BlockSpec((tm, ROPE_DIM), row)),
        compiler_params=_params("parallel"),
        name="mla_q_proj",
    )(x, w_lat, gq, gkv, w_uq_all, cos, sin)


def _mla_kv_proj_kernel(ckv_ref, kr_ref, wukv_ref, kt_ref, v_ref):
    cb = ckv_ref[...].astype(BF16)
    kr = kr_ref[...]
    kr_t = jnp.concatenate([kr, kr], axis=1).T[0:ROPE_DIM, :].astype(BF16)
    n_k = MLA_HEADS * QK_NOPE_DIM
    for h in range(MLA_HEADS):
        kn = _dot(cb, wukv_ref[:, h * QK_NOPE_DIM:(h + 1) * QK_NOPE_DIM])
        kt_ref[h, 0:QK_NOPE_DIM, :] = kn.T.astype(BF16)
        kt_ref[h, QK_NOPE_DIM:QK_DIM, :] = kr_t
        vv = _dot(cb, wukv_ref[:, n_k + h * V_HEAD_DIM:n_k + (h + 1) * V_HEAD_DIM])
        v_ref[h] = vv.astype(BF16)


def _mla_kv_proj(ckv, kr, w_ukv_all, tm):
    m, kv_lora = ckv.shape
    row = lambda i: (i, 0)
    return pl.pallas_call(
        _mla_kv_proj_kernel,
        out_shape=(jax.ShapeDtypeStruct((MLA_HEADS, m // tm, QK_DIM, tm), BF16),
                   jax.ShapeDtypeStruct((MLA_HEADS, m, V_HEAD_DIM), BF16)),
        grid=(m // tm,),
        in_specs=[pl.BlockSpec((tm, kv_lora), row),
                  pl.BlockSpec((tm, ROPE_DIM), row),
                  pl.BlockSpec(w_ukv_all.shape, lambda i: (0, 0))],
        out_specs=(pl.BlockSpec((MLA_HEADS, None, QK_DIM, tm), lambda i: (0, i, 0, 0)),
                   pl.BlockSpec((MLA_HEADS, tm, V_HEAD_DIM), lambda i: (0, i, 0))),
        compiler_params=_params("parallel"),
        name="mla_kv_proj",
    )(ckv, kr, w_ukv_all)


def _gates_kernel(x_ref, w_ref, o_ref):
    z = _dot(x_ref[...].astype(BF16), w_ref[...])
    o_ref[...] = (1.0 / (1.0 + jnp.exp(-z))).astype(o_ref.dtype)


def _gates(x, w_g, tm=1024, tn=1024):
    m, d = x.shape
    n = w_g.shape[1]
    tm = min(tm, m)
    return pl.pallas_call(
        _gates_kernel,
        out_shape=jax.ShapeDtypeStruct((m, n), BF16),
        grid=(m // tm, n // tn),
        in_specs=[pl.BlockSpec((tm, d), lambda i, j: (i, 0)),
                  pl.BlockSpec((d, tn), lambda i, j: (0, j))],
        out_specs=pl.BlockSpec((tm, tn), lambda i, j: (i, j)),
        compiler_params=_params("parallel", "arbitrary"),
        name="gates",
    )(x, w_g)


def _sb_attn_kernel(q_ref, k_ref, vt_ref, o_ref, *, tq, tk, q_off, n_heads):
    i = pl.program_id(2)
    t0 = q_off + i * tq
    hd = SB_HEAD_DIM
    heads = range(n_heads)
    qs = [q_ref[:, h * hd:(h + 1) * hd] for h in heads]
    rel = lax.broadcasted_iota(jnp.int32, (tk, tq), 0) - lax.broadcasted_iota(jnp.int32, (tk, tq), 1)
    r = lax.broadcasted_iota(jnp.int32, (tk, tk), 0)
    c = lax.broadcasted_iota(jnp.int32, (tk, tk), 1)
    upper = (c >= r).astype(BF16)

    def cond(carry):
        j, runs, _ = carry
        lowest = functools.reduce(jnp.minimum, runs)
        return jnp.logical_and(j >= 0, jnp.min(lowest) < SB_SKIP)

    def body(carry):
        j, runs, accs = carry
        k0 = pl.multiple_of(j * tk, tk)
        mask = rel < (t0 - k0)
        new_runs, new_accs = [], []
        for h in heads:
            z = _dot_nt(k_ref[pl.ds(k0, tk), h * hd:(h + 1) * hd], qs[h])
            sp = jnp.maximum(z, 0.0) + jnp.log(1.0 + jnp.exp(-jnp.abs(z)))
            sp = jnp.where(mask, sp, 0.0)
            hi = sp.astype(BF16)
            lo = (sp - hi.astype(F32)).astype(BF16)
            cs = _dot(upper, hi) + _dot(upper, lo) + runs[h]
            a = jnp.where(mask, jnp.exp(z - cs), 0.0)
            new_accs.append(accs[h] + _dot(vt_ref[h, j], a.astype(BF16)))
            new_runs.append(cs[0:1, :])
        return j - 1, tuple(new_runs), tuple(new_accs)

    j_start = (t0 + tq - 2) // tk
    init = (j_start, tuple(jnp.zeros((1, tq), F32) for _ in heads),
            tuple(jnp.zeros((hd, tq), F32) for _ in heads))
    _, _, accs = lax.while_loop(cond, body, init)
    for h in heads:
        o_ref[:, h * hd:(h + 1) * hd] = accs[h].T.astype(o_ref.dtype)


def _sb_attn(q, k, vt, *, n_batch, tq_total, q_off, tq, tk, n_heads=4):
    tk_total = k.shape[0] // n_batch
    nq = tq_total // tq
    hw = n_heads * SB_HEAD_DIM
    return pl.pallas_call(
        functools.partial(_sb_attn_kernel, tq=tq, tk=tk, q_off=q_off, n_heads=n_heads),
        out_shape=jax.ShapeDtypeStruct((n_batch * tq_total, SB_HEADS * SB_HEAD_DIM), BF16),
        grid=(n_batch, SB_HEADS // n_heads, nq),
        in_specs=[pl.BlockSpec((tq, hw), lambda b, g, i: (b * nq + i, g)),
                  pl.BlockSpec((tk_total, hw), lambda b, g, i: (b, g)),
                  pl.BlockSpec((None, n_heads, tk_total // tk, SB_HEAD_DIM, tk),
                               lambda b, g, i: (b, g, 0, 0, 0))],
        out_specs=pl.BlockSpec((tq, hw), lambda b, g, i: (b * nq + i, g)),
        compiler_params=_params("parallel", "parallel", "arbitrary"),
        name="sb_attn",
    )(q, k, vt)


def _sb_decode_kernel(q_ref, kn_ref, vn_ref, kc_ref, vc_ref, o_ref, *, dec, tk):
    hd = SB_HEAD_DIM
    heads = range(SB_HEADS)
    rows = SB_HEADS * dec
    past = kc_ref.shape[0] // SB_HEADS

    def softplus(z):
        return jnp.maximum(z, 0.0) + jnp.log(1.0 + jnp.exp(-jnp.abs(z)))

    def lower(n):
        return (lax.broadcasted_iota(jnp.int32, (n, n), 0) >= lax.broadcasted_iota(jnp.int32, (n, n), 1)).astype(BF16)

    def reverse_cumsum(sp, tri):
        hi = sp.astype(BF16)
        lo = (sp - hi.astype(F32)).astype(BF16)
        return _dot(hi, tri) + _dot(lo, tri)

    def head_cols(x, h):
        return x[:, h * hd:(h + 1) * hd]

    qs = [head_cols(q_ref[...], h) for h in heads]

    kn = kn_ref[...].astype(BF16)
    vn = vn_ref[...].astype(BF16)
    z = jnp.concatenate([_dot_nt(qs[h], head_cols(kn, h)) for h in heads], axis=0)
    qt = lax.rem(lax.broadcasted_iota(jnp.int32, (rows, dec), 0), dec)
    mask = lax.broadcasted_iota(jnp.int32, (rows, dec), 1) < qt
    cs = reverse_cumsum(jnp.where(mask, softplus(z), 0.0), lower(dec))
    a = jnp.where(mask, jnp.exp(z - cs), 0.0).astype(BF16)
    acc0 = jnp.concatenate([_dot(a[h * dec:(h + 1) * dec, :], head_cols(vn, h)) for h in heads], axis=0)
    tri = lower(tk)

    def cond(carry):
        j, run, _ = carry
        return jnp.logical_and(j >= 0, jnp.min(run) < SB_SKIP)

    def body(carry):
        j, run, acc = carry
        r0 = pl.multiple_of(j * (tk * SB_HEADS), tk * SB_HEADS)

        def head_rows(ref, h):
            return ref[pl.ds(r0 + h, tk, stride=SB_HEADS), :].astype(BF16)

        z = jnp.concatenate([_dot_nt(qs[h], head_rows(kc_ref, h)) for h in heads], axis=0)
        cs = reverse_cumsum(softplus(z), tri) + run
        a = jnp.exp(z - cs).astype(BF16)
        pv = jnp.concatenate([_dot(a[h * dec:(h + 1) * dec, :], head_rows(vc_ref, h)) for h in heads], axis=0)
        return j - 1, cs[:, 0:1], acc + pv

    _, _, acc = lax.while_loop(cond, body, (past // tk - 1, cs[:, 0:1], acc0))
    for h in heads:
        o_ref[:, h * hd:(h + 1) * hd] = acc[h * dec:(h + 1) * dec, :].astype(o_ref.dtype)


def _sb_decode(q, k_new, v_new, cache_k, cache_v, *, dec, tk=256):
    nd, rows, hd = cache_k.shape
    past = rows // SB_HEADS
    assert past % tk == 0 and hd == SB_HEAD_DIM
    width = SB_HEADS * hd
    new = pl.BlockSpec((dec, width), lambda b: (b, 0))
    cache = pl.BlockSpec((None, rows, hd), lambda b: (b, 0, 0))
    return pl.pallas_call(
        functools.partial(_sb_decode_kernel, dec=dec, tk=tk),
        out_shape=jax.ShapeDtypeStruct((nd * dec, width), BF16),
        grid=(nd,),
        in_specs=[new, new, new, cache, cache],
        out_specs=new,
        compiler_params=_params("parallel"),
        name="sb_decode",
    )(q, k_new, v_new, cache_k, cache_v)


def _mla_attn_kernel(q_ref, kt_ref, v_ref, o_ref, s_scr, p_scr, acc_scr, m_scr, *, t):
    i = pl.program_id(2)
    ones = jnp.ones((t, V_HEAD_DIM), BF16)
    heads = range(q_ref.shape[0])

    def scores(h, j):
        return _dot(q_ref[h], kt_ref[h, j])

    def weighted_values(h, p, j):
        v_ext = jnp.concatenate([v_ref[h, pl.ds(pl.multiple_of(j * t, t), t), :], ones], axis=1)
        return _dot(p, v_ext)

    def step(j, a, diagonal=False):
        for h in heads:
            if not diagonal:
                s_scr[h, 1 - a] = scores(h, j + 1)
            pv = weighted_values(h, p_scr[h, 1 - a], jnp.maximum(j - 1, 0))
            s = s_scr[h, a]
            if diagonal:
                qchunk = lax.broadcasted_iota(jnp.int32, (t, t), 0) // CHUNK
                kchunk = lax.broadcasted_iota(jnp.int32, (t, t), 1) // CHUNK
                s = jnp.where(kchunk <= qchunk, s, NEG_INF)
            m = m_scr[h]
            m_new = jnp.maximum(m, jnp.max(s, axis=1, keepdims=True))
            p = jnp.exp2(s - m_new).astype(BF16)
            acc = jnp.exp2(m - m_new) * (acc_scr[h] + pv)
            if diagonal:
                acc = acc + weighted_values(h, p, j)
                o = acc[:, :V_HEAD_DIM] / acc[:, V_HEAD_DIM:]
                o_ref[:, h * V_HEAD_DIM:(h + 1) * V_HEAD_DIM] = o.astype(o_ref.dtype)
            else:
                m_scr[h] = m_new
                p_scr[h, a] = p
                acc_scr[h] = acc

    for h in heads:
        s_scr[h, 0] = scores(h, 0)
        p_scr[h, 1] = jnp.zeros((t, t), BF16)
    acc_scr[...] = jnp.zeros_like(acc_scr)
    m_scr[...] = jnp.full(m_scr.shape, NEG_INF, F32)

    def pair(jj, _):
        step(2 * jj, 0)
        step(2 * jj + 1, 1)
        return 0

    lax.fori_loop(0, i // 2, pair, 0)
    odd = lax.rem(i, 2) == 1

    @pl.when(odd)
    def _():
        step(i - 1, 0)
        step(i, 1, diagonal=True)

    @pl.when(jnp.logical_not(odd))
    def _():
        step(i, 0, diagonal=True)


def _mla_attn(q_cat, kt, v, *, n_batch, seq, hb=2):
    t = kt.shape[3]
    assert seq % t == 0 and t % CHUNK == 0 and MLA_HEADS % hb == 0
    nq = seq // t
    return pl.pallas_call(
        functools.partial(_mla_attn_kernel, t=t),
        out_shape=jax.ShapeDtypeStruct((n_batch * seq, MLA_HEADS * V_HEAD_DIM), BF16),
        grid=(n_batch, MLA_HEADS // hb, nq),
        in_specs=[pl.BlockSpec((hb, t, QK_DIM), lambda b, g, i: (g, b * nq + i, 0)),
                  pl.BlockSpec((hb, nq, QK_DIM, t), lambda b, g, i: (g, b, 0, 0)),
                  pl.BlockSpec((hb, seq, V_HEAD_DIM), lambda b, g, i: (g, b, 0))],
        out_specs=pl.BlockSpec((t, hb * V_HEAD_DIM), lambda b, g, i: (b * nq + i, g)),
        scratch_shapes=[pltpu.VMEM((hb, 2, t, t), F32), pltpu.VMEM((hb, 2, t, t), BF16),
                        pltpu.VMEM((hb, t, 2 * V_HEAD_DIM), F32), pltpu.VMEM((hb, t, 1), F32)],
        compiler_params=_params("parallel", "parallel", "arbitrary"),
        name="mla_attn",
    )(q_cat, kt, v)


def _mla_decode_kernel(q_ref, wuk_ref, wuv_ref, cckv_ref, ckr_ref, nckv_ref, nkr_ref, o_ref, *, past, dec):
    rows = MLA_HEADS * dec
    ql = jnp.concatenate([_dot(q_ref[h, :, 0:QK_NOPE_DIM], wuk_ref[h]) for h in range(MLA_HEADS)], axis=0)
    ql = ql.astype(BF16)
    qr = jnp.concatenate([q_ref[h, :, QK_NOPE_DIM:QK_DIM] for h in range(MLA_HEADS)], axis=0)
    kc = cckv_ref[...].astype(BF16)
    kn = nckv_ref[...].astype(BF16)
    s_c = _dot_nt(ql, kc) + _dot_nt(qr, ckr_ref[...].astype(BF16))
    s_n = _dot_nt(ql, kn) + _dot_nt(qr, nkr_ref[...].astype(BF16))
    qpos = past + lax.rem(lax.broadcasted_iota(jnp.int32, (rows, dec), 0), dec)
    kpos = past + lax.broadcasted_iota(jnp.int32, (rows, dec), 1)
    s_n = jnp.where(kpos // CHUNK <= qpos // CHUNK, s_n, NEG_INF)
    m = jnp.maximum(jnp.max(s_c, axis=1, keepdims=True), jnp.max(s_n, axis=1, keepdims=True))
    p_c = jnp.exp2(s_c - m)
    p_n = jnp.exp2(s_n - m)
    l = jnp.sum(p_c, axis=1, keepdims=True) + jnp.sum(p_n, axis=1, keepdims=True)
    ol = (_dot(p_c.astype(BF16), kc) + _dot(p_n.astype(BF16), kn)) / l
    ol = ol.astype(BF16)
    for h in range(MLA_HEADS):
        o_h = _dot(ol[h * dec:(h + 1) * dec, :], wuv_ref[h])
        o_ref[:, h * V_HEAD_DIM:(h + 1) * V_HEAD_DIM] = o_h.astype(o_ref.dtype)


def _mla_decode(q_cat, w_uk, w_uv, cache_ckv, cache_kr, ckv, kr, *, q_blk0, dec):
    nd, past, kv_lora = cache_ckv.shape
    const3 = lambda b: (0, 0, 0)
    return pl.pallas_call(
        functools.partial(_mla_decode_kernel, past=past, dec=dec),
        out_shape=jax.ShapeDtypeStruct((nd * dec, MLA_HEADS * V_HEAD_DIM), BF16),
        grid=(nd,),
        in_specs=[pl.BlockSpec((MLA_HEADS, dec, QK_DIM), lambda b: (0, q_blk0 + b, 0)),
                  pl.BlockSpec(w_uk.shape, const3),
                  pl.BlockSpec(w_uv.shape, const3),
                  pl.BlockSpec((None, past, kv_lora), lambda b: (b, 0, 0)),
                  pl.BlockSpec((None, past, ROPE_DIM), lambda b: (b, 0, 0)),
                  pl.BlockSpec((dec, kv_lora), lambda b: (q_blk0 + b, 0)),
                  pl.BlockSpec((dec, ROPE_DIM), lambda b: (q_blk0 + b, 0))],
        out_specs=pl.BlockSpec((dec, MLA_HEADS * V_HEAD_DIM), lambda b: (b, 0)),
        compiler_params=_params("parallel"),
        name="mla_decode",
    )(q_cat, w_uk, w_uv, cache_ckv, cache_kr, ckv, kr)


def _merge_kernel(x_ref, osb_ref, omla_ref, gates_ref, wa_ref, wb_ref, wo_ref, g_ref, b_ref, *rest,
                  alpha, n_own):
    o_ref, u_ref = rest[-2:]
    i = pl.program_id(0)
    d = o_ref.shape[1]

    @pl.when(i < n_own)
    def _():
        osb = osb_ref[...]
        omla = omla_ref[...]
        for c in range(0, d, COL_CHUNK):
            ga = gates_ref[:, c:c + COL_CHUNK].astype(F32)
            gb = gates_ref[:, d + c:d + c + COL_CHUNK].astype(F32)
            u = ga * _dot(osb, wa_ref[:, c:c + COL_CHUNK]) + gb * _dot(omla, wb_ref[:, c:c + COL_CHUNK])
            u_ref[:, c:c + COL_CHUNK] = u.astype(BF16)
        mix = _dot(u_ref[...], wo_ref[...])
        o_ref[...] = _layer_norm(alpha * x_ref[...] + mix, g_ref[...], b_ref[...])

    @pl.when(i >= n_own)
    def _():
        o_ref[...] = jnp.zeros_like(o_ref)


def _merge(x, o_sb, o_mla, gates, w_br_a, w_br_b, w_o, ln_g, ln_b, *, alpha, out_rows, row0=0, into=None,
           tm=ROW_TILE):
    m, d = x.shape
    blk0 = row0 // tm
    n_own = m // tm
    n_rows_tiles = n_own if into is not None else out_rows // tm
    own = lambda i: (jnp.minimum(i, n_own - 1), 0)
    const = lambda i: (0, 0)
    resident = lambda w: pl.BlockSpec(w.shape, const, pipeline_mode=pl.Buffered(1))
    extra_specs = [] if into is None else [pl.BlockSpec(memory_space=pl.ANY)]
    extra_args = [] if into is None else [into]
    return pl.pallas_call(
        functools.partial(_merge_kernel, alpha=alpha, n_own=n_own),
        out_shape=jax.ShapeDtypeStruct((out_rows, d), F32),
        grid=(n_rows_tiles,),
        input_output_aliases={} if into is None else {9: 0},
        in_specs=[pl.BlockSpec((tm, d), own),
                  pl.BlockSpec((tm, o_sb.shape[1]), own),
                  pl.BlockSpec((tm, o_mla.shape[1]), own),
                  pl.BlockSpec((tm, 2 * d), own),
                  resident(w_br_a), resident(w_br_b), resident(w_o),
                  pl.BlockSpec((1, d), const),
                  pl.BlockSpec((1, d), const)] + extra_specs,
        out_specs=pl.BlockSpec((tm, d), lambda i: (blk0 + i, 0)),
        scratch_shapes=[pltpu.VMEM((tm, d), BF16)],
        compiler_params=_params("parallel"),
        name="merge_ln1",
    )(x, o_sb, o_mla, gates, w_br_a, w_br_b, w_o, ln_g, ln_b, *extra_args)


def _router_kernel(x_ref, wh_ref, wl_ref, o_ref):
    x = x_ref[...]
    xh = x.astype(BF16)
    xl = (x - xh.astype(F32)).astype(BF16)
    logits = _dot(xh, wh_ref[...]) + (_dot(xh, wl_ref[...]) + _dot(xl, wh_ref[...]))
    lane = lax.broadcasted_iota(jnp.int32, logits.shape, 1)
    big = jnp.int32(LANES)
    is_grp = lane < N_GROUPS
    gl = jnp.where(is_grp, logits, -jnp.inf)
    gmax = jnp.max(gl, axis=1, keepdims=True)
    gsum = jnp.sum(jnp.where(is_grp, jnp.exp(gl - gmax), 0.0), axis=1, keepdims=True)
    p_grp = 1.0 / gsum
    grp = jnp.min(jnp.where(jnp.logical_and(is_grp, gl == gmax), lane, big), axis=1, keepdims=True)
    lo = N_GROUPS + grp * EXPERTS_PER_GROUP
    in_grp = jnp.logical_and(lane >= lo, lane < lo + EXPERTS_PER_GROUP)
    el = jnp.where(in_grp, logits, -jnp.inf)
    v1 = jnp.max(el, axis=1, keepdims=True)
    i1 = jnp.min(jnp.where(el == v1, lane, big), axis=1, keepdims=True)
    el2 = jnp.where(lane == i1, -jnp.inf, el)
    v2 = jnp.max(el2, axis=1, keepdims=True)
    i2 = jnp.min(jnp.where(el2 == v2, lane, big), axis=1, keepdims=True)
    e2w = jnp.exp(v2 - v1)
    w1 = p_grp / (1.0 + e2w)
    w2 = p_grp * e2w / (1.0 + e2w)
    out = jnp.where(lane == 0, (i1 - N_GROUPS).astype(F32),
                    jnp.where(lane == 1, (i2 - N_GROUPS).astype(F32),
                              jnp.where(lane == 2, w1, jnp.where(lane == 3, w2, 0.0))))
    o_ref[...] = out


def _router(x1, w_hi, w_lo, tm=ROW_TILE):
    m, d = x1.shape
    return pl.pallas_call(
        _router_kernel,
        out_shape=jax.ShapeDtypeStruct((m, LANES), F32),
        grid=(m // tm,),
        in_specs=[pl.BlockSpec((tm, d), lambda i: (i, 0)),
                  pl.BlockSpec((d, LANES), lambda i: (0, 0)),
                  pl.BlockSpec((d, LANES), lambda i: (0, 0))],
        out_specs=pl.BlockSpec((tm, LANES), lambda i: (i, 0)),
        compiler_params=_params("parallel"),
        name="router",
    )(x1, w_hi, w_lo)


def _row_gather_start(idx_ref, src_hbm, dst_ref, sem, n_rows, unrolled=False, priorities=(0,)):
    def start(r, priority):
        tok = idx_ref[0, 0, r]
        pltpu.make_async_copy(src_hbm.at[pl.ds(tok, 1)], dst_ref.at[pl.ds(r, 1)], sem).start(priority=priority)

    if unrolled:
        for r in range(n_rows):
            start(r, priorities[r % len(priorities)])
    else:
        def body(r, _):
            start(r, 0)
            return 0
        lax.fori_loop(0, n_rows, body, 0)


def _row_gather_wait(src_hbm, dst_ref, sem, n_rows):
    pltpu.make_async_copy(src_hbm.at[pl.ds(0, n_rows)], dst_ref, sem).wait()


GATHER_RING = 3
COL_CHUNK = 512


def _moe_kernel(te_ref, nu_ref, ws_ref, nx_ref, idx_ref, idx1_ref, idx2_ref, x_hbm, wg_hbm, wu_hbm, wd_hbm,
                rw_ref, y_ref, xbuf0, xbuf1, xbuf2, sem, wg_f, wu_f, wd_f, wsem, wgb, wub, wdb, *, tm):
    t = pl.program_id(0)
    n_used = nu_ref[0]
    bufs = (xbuf0, xbuf1, xbuf2)

    def weight_copies(e, slot):
        return [pltpu.make_async_copy(w_hbm.at[e], w_f.at[slot], wsem.at[slot])
                for w_hbm, w_f in ((wg_hbm, wg_f), (wu_hbm, wu_f), (wd_hbm, wd_f))]

    @pl.when(t == 0)
    def _():
        _row_gather_start(idx_ref, x_hbm, xbuf0, sem.at[0], tm)
        _row_gather_start(idx1_ref, x_hbm, xbuf1, sem.at[1], tm)
        for cp in weight_copies(te_ref[0], 0):
            cp.start(priority=1)

    first_of_expert = jnp.logical_or(t == 0, te_ref[t] != te_ref[jnp.maximum(t - 1, 0)])

    @pl.when(jnp.logical_and(first_of_expert, t < n_used))
    def _():
        slot = ws_ref[t]
        for cp in weight_copies(te_ref[t], slot):
            cp.wait()

        @pl.when(nx_ref[t] >= 0)
        def _():
            for cp in weight_copies(nx_ref[t], 1 - slot):
                cp.start(priority=1)

        wgb[...] = wg_f[slot].astype(BF16)
        wub[...] = wu_f[slot].astype(BF16)
        wdb[...] = wd_f[slot].astype(BF16)

    def run(a):
        ahead = (a + 2) % GATHER_RING

        @pl.when(t <= n_used + 1)
        def _():
            _row_gather_wait(x_hbm, bufs[a], sem.at[a], tm)

        @pl.when(t < n_used)
        def _():
            _row_gather_start(idx2_ref, x_hbm, bufs[ahead], sem.at[ahead], tm, unrolled=True, priorities=(1,))
            xb = bufs[a][...].astype(BF16)
            g = _dot(xb, wgb[...])
            u = _dot(xb, wub[...])
            h = (g / (1.0 + jnp.exp(-g))) * u
            y_ref[...] = _dot(h.astype(BF16), wdb[...]) * rw_ref[:, 0:1]

    slot = lax.rem(t, GATHER_RING)
    for a in range(GATHER_RING):
        pl.when(slot == a)(functools.partial(run, a))

    @pl.when(t >= n_used)
    def _():
        y_ref[...] = jnp.zeros_like(y_ref)


def _moe(tile_expert, n_used, w_slot, next_expert, row_token, x1, w_gate, w_up, w_down, row_w, *, tm):
    n_tiles = tile_expert.shape[0]
    d = x1.shape[1]
    f = w_gate.shape[2]
    idx3 = row_token.reshape(n_tiles, 1, tm)
    ahead = lambda k: pl.BlockSpec((1, 1, tm), lambda t, *_: (jnp.minimum(t + k, n_tiles - 1), 0, 0),
                                   memory_space=pltpu.SMEM)
    hbm = pl.BlockSpec(memory_space=pl.ANY)
    grid_spec = pltpu.PrefetchScalarGridSpec(
        num_scalar_prefetch=4,
        grid=(n_tiles,),
        in_specs=[ahead(0), ahead(1), ahead(2), hbm, hbm, hbm, hbm,
                  pl.BlockSpec((tm, LANES), lambda t, *_: (t, 0))],
        out_specs=pl.BlockSpec((tm, d), lambda t, *_: (t, 0)),
        scratch_shapes=[pltpu.VMEM((tm, d), F32) for _ in range(GATHER_RING)] + [
                        pltpu.SemaphoreType.DMA((GATHER_RING,)),
                        pltpu.VMEM((2, d, f), F32),
                        pltpu.VMEM((2, d, f), F32),
                        pltpu.VMEM((2, f, d), F32),
                        pltpu.SemaphoreType.DMA((2,)),
                        pltpu.VMEM((d, f), BF16),
                        pltpu.VMEM((d, f), BF16),
                        pltpu.VMEM((f, d), BF16)],
    )
    return pl.pallas_call(
        functools.partial(_moe_kernel, tm=tm),
        out_shape=jax.ShapeDtypeStruct((n_tiles * tm, d), F32),
        grid_spec=grid_spec,
        compiler_params=_params("arbitrary"),
        name="moe_experts",
    )(tile_expert, n_used, w_slot, next_expert, idx3, idx3, idx3, x1, w_gate, w_up, w_down, row_w)


def _final_kernel(p1_ref, p2_ref, p1a_ref, p2a_ref, p1b_ref, p2b_ref, y_hbm, x1_ref, pe_ref, wpp_ref, wpg_ref,
                  g_ref, b_ref, o_ref, ya0, yb0, ya1, yb1, ya2, yb2, sem, *, tm, alpha):
    t = pl.program_id(0)
    last = pl.num_programs(0) - 1
    bufs = ((ya0, yb0), (ya1, yb1), (ya2, yb2))

    def start(pa_ref, pb_ref, s, unrolled):
        _row_gather_start(pa_ref, y_hbm, bufs[s][0], sem.at[s], tm, unrolled=unrolled, priorities=(1,))
        _row_gather_start(pb_ref, y_hbm, bufs[s][1], sem.at[s], tm, unrolled=unrolled, priorities=(1,))

    def wait(s):
        _row_gather_wait(y_hbm, bufs[s][0], sem.at[s], tm)
        _row_gather_wait(y_hbm, bufs[s][1], sem.at[s], tm)

    @pl.when(t == 0)
    def _():
        start(p1_ref, p2_ref, 0, False)
        start(p1a_ref, p2a_ref, 1, False)

    def run(a):
        wait(a)
        start(p1b_ref, p2b_ref, (a + 2) % GATHER_RING, True)
        f = bufs[a][0][...] + bufs[a][1][...]
        x2 = _layer_norm(alpha * x1_ref[...] + f, g_ref[...], b_ref[...])
        o_ref[...] = x2
        x2b = x2.astype(BF16)
        peb = pe_ref[...].astype(BF16)
        n = o_ref.shape[1]
        for c in range(0, n, COL_CHUNK):
            gate = 1.0 / (1.0 + jnp.exp(-_dot(x2b, wpg_ref[:, c:c + COL_CHUNK])))
            o_ref[:, c:c + COL_CHUNK] += gate * _dot(peb, wpp_ref[:, c:c + COL_CHUNK])

        @pl.when(t == last)
        def _():
            wait((a + 1) % GATHER_RING)
            wait((a + 2) % GATHER_RING)

    slot = lax.rem(t, GATHER_RING)
    for a in range(GATHER_RING):
        pl.when(slot == a)(functools.partial(run, a))


def _final(pos1, pos2, y_sorted, x1, x1_blk0, pe, w_pp, w_pg, ln_g, ln_b, *, alpha, tm=ROW_TILE):
    d = x1.shape[1]
    m = pe.shape[0]
    n_tiles = m // tm
    p1 = pos1.reshape(n_tiles, 1, tm)
    p2 = pos2.reshape(n_tiles, 1, tm)
    ahead = lambda k: pl.BlockSpec((1, 1, tm), lambda t: (jnp.minimum(t + k, n_tiles - 1), 0, 0),
                                   memory_space=pltpu.SMEM)
    const = lambda t: (0, 0)
    return pl.pallas_call(
        functools.partial(_final_kernel, tm=tm, alpha=alpha),
        out_shape=jax.ShapeDtypeStruct((m, d), F32),
        grid=(n_tiles,),
        in_specs=[ahead(0), ahead(0), ahead(1), ahead(1), ahead(2), ahead(2),
                  pl.BlockSpec(memory_space=pl.ANY),
                  pl.BlockSpec((tm, d), lambda t: (x1_blk0 + t, 0)),
                  pl.BlockSpec((tm, pe.shape[1]), lambda t: (t, 0)),
                  pl.BlockSpec(w_pp.shape, const),
                  pl.BlockSpec(w_pg.shape, const),
                  pl.BlockSpec((1, d), const),
                  pl.BlockSpec((1, d), const)],
        out_specs=pl.BlockSpec((tm, d), lambda t: (t, 0)),
        scratch_shapes=[pltpu.VMEM((tm, d), F32) for _ in range(2 * GATHER_RING)] + [
                        pltpu.SemaphoreType.DMA((GATHER_RING,))],
        compiler_params=_params("arbitrary"),
        name="combine_ln2_ple",
    )(p1, p2, p1, p2, p1, p2, y_sorted, x1, pe, w_pp, w_pg, ln_g, ln_b)


def _rotate_half_cols(w):
    half = ROPE_DIM // 2
    return jnp.concatenate([-w[..., half:], w[..., :half]], axis=-1)


def _rope_tables(pos):
    inv_freq = 1.0 / (ROPE_THETA ** (jnp.arange(0, ROPE_DIM, 2, dtype=F32) / ROPE_DIM))
    ang = pos.astype(F32)[:, None] * inv_freq[None, :]
    ang = jnp.concatenate([ang, ang, ang, ang], axis=-1)
    return jnp.cos(ang), jnp.sin(ang)


def _route_tables(route, tm):
    m = route.shape[0]
    e_pair = jnp.concatenate([route[:, 0], route[:, 1]]).astype(jnp.int32)
    w_pair = jnp.concatenate([route[:, 2], route[:, 3]])
    tok = jnp.arange(m, dtype=jnp.int32)
    tok_pair = jnp.concatenate([tok, tok])
    onehot = (e_pair[:, None] == jnp.arange(N_EXPERTS, dtype=jnp.int32)[None, :]).astype(jnp.int32)
    csum = jnp.cumsum(onehot, axis=0)
    rank = jnp.sum(csum * onehot, axis=1) - 1
    counts = csum[-1]
    padded = ((counts + tm - 1) // tm) * tm
    pad_end = jnp.cumsum(padded)
    pad_off = pad_end - padded
    pos = pad_off[e_pair] + rank
    n_rows = (2 * m // tm + N_EXPERTS - 1 + GATHER_RING - 1) * tm
    payload = jnp.stack([tok_pair, lax.bitcast_convert_type(w_pair, jnp.int32)], axis=1)
    rows = jnp.zeros((n_rows, 2), jnp.int32).at[pos].set(payload)
    row_token = rows[:, 0]
    row_w = lax.bitcast_convert_type(rows[:, 1], F32)
    tile_start = jnp.arange(n_rows // tm, dtype=jnp.int32) * tm
    tile_expert = jnp.sum((pad_end[None, :] <= tile_start[:, None]).astype(jnp.int32), axis=1)
    tile_expert = jnp.minimum(tile_expert, N_EXPERTS - 1)
    n_used = (pad_end[-1] // tm).astype(jnp.int32).reshape(1)
    last_expert = tile_expert[jnp.maximum(n_used[0] - 1, 0)]
    tile_expert = jnp.where(tile_start // tm < n_used[0], tile_expert, last_expert).astype(jnp.int32)
    changed = jnp.concatenate([jnp.zeros((1,), jnp.int32),
                               (tile_expert[1:] != tile_expert[:-1]).astype(jnp.int32)])
    w_slot = (jnp.cumsum(changed) % 2).astype(jnp.int32)
    experts = jnp.arange(N_EXPERTS, dtype=jnp.int32)
    later_nonempty = jnp.logical_and(experts[None, :] > experts[:, None], counts[None, :] > 0)
    next_nonempty = jnp.min(jnp.where(later_nonempty, experts[None, :], N_EXPERTS), axis=1)
    next_nonempty = jnp.where(next_nonempty == N_EXPERTS, -1, next_nonempty).astype(jnp.int32)
    next_expert = next_nonempty[tile_expert]
    row_w = jnp.broadcast_to(row_w[:, None], (n_rows, LANES))
    return tile_expert, n_used, w_slot, next_expert, row_token, row_w, pos[:m], pos[m:]


def kernel(x_prompt, x_sample, cache_sb_k, cache_sb_v, cache_mla_ckv, cache_mla_krope, p_prompt, p_sample,
           w_in, q_a_norm, kv_a_norm, w_uq, w_ukv, w_br_a, w_br_b, w_o, ln1_g, ln1_b, ln2_g, ln2_b,
           w_router_group, w_router_expert, w_exp_gate, w_exp_up, w_exp_down, w_ple_proj, w_ple_gate):
    depth = w_in.shape[0]
    assert depth == 1, "single trunk layer"
    nb, seq, d = x_prompt.shape
    nd, dec, _ = x_sample.shape
    past = cache_sb_k.shape[2]
    sbw = SB_HEADS * SB_HEAD_DIM
    q_lora = q_a_norm.shape[1]
    kv_lora = kv_a_norm.shape[1]
    alpha = (2.0 * depth) ** 0.25
    mp = nb * seq
    ms = nd * dec

    w0 = w_in[0]
    o1 = 3 * sbw
    o2 = o1 + q_lora
    o3 = o2 + kv_lora
    o4 = o3 + ROPE_DIM
    w_sb = w0[:, :o1].astype(BF16)
    w_kr = w0[:, o3:o4]
    w_lat = jnp.concatenate([w0[:, o1:o3], w_kr, _rotate_half_cols(w_kr)], axis=1).astype(BF16)
    w_g = w0[:, o4:].astype(BF16)
    wq = w_uq[0].reshape(q_lora, MLA_HEADS, QK_DIM)
    wq_r = wq[:, :, QK_NOPE_DIM:]
    w_uq_all = jnp.concatenate([wq[:, :, :QK_NOPE_DIM].reshape(q_lora, -1), wq_r.reshape(q_lora, -1),
                                _rotate_half_cols(wq_r).reshape(q_lora, -1)], axis=1).astype(BF16)
    wkv = w_ukv[0].reshape(kv_lora, MLA_HEADS, QK_NOPE_DIM + V_HEAD_DIM)
    w_ukv_all = jnp.concatenate([wkv[:, :, :QK_NOPE_DIM].reshape(kv_lora, -1),
                                 wkv[:, :, QK_NOPE_DIM:].reshape(kv_lora, -1)], axis=1).astype(BF16)
    w_r = jnp.concatenate([w_router_group[0], w_router_expert[0]], axis=1)
    w_r = jnp.pad(w_r, ((0, 0), (0, LANES - w_r.shape[1])))
    w_r_hi = w_r.astype(BF16)
    w_r_lo = (w_r - w_r_hi.astype(F32)).astype(BF16)

    w_uk = jnp.transpose(wkv[:, :, :QK_NOPE_DIM], (1, 2, 0)).astype(BF16)
    w_uv = jnp.transpose(wkv[:, :, QK_NOPE_DIM:], (1, 0, 2)).astype(BF16)
    w_a, w_b, w_out = w_br_a[0].astype(BF16), w_br_b[0].astype(BF16), w_o[0].astype(BF16)
    w_pp, w_pg = w_ple_proj[0].astype(BF16), w_ple_gate[0].astype(BF16)

    xp = x_prompt.reshape(mp, d)
    cos_p, sin_p = _rope_tables(jnp.arange(seq, dtype=jnp.int32))
    tile = 256
    sbq_p, sbk_p, sbv_p, sbk_b, vt_p = _sb_proj(xp, w_sb, seq=seq, tm=tile)
    q_cat_p, ckv_p, kr_p = _mla_q_proj(xp, w_lat, q_a_norm, kv_a_norm, w_uq_all, cos_p, sin_p)
    gates_p = _gates(xp, w_g)
    o_sb_p = _sb_attn(sbq_p, sbk_b, vt_p, n_batch=nb, tq_total=seq, q_off=0, tq=tile, tk=tile)
    kt_mla, v_mla = _mla_kv_proj(ckv_p, kr_p, w_ukv_all, tm=512)
    o_mla_p = _mla_attn(q_cat_p, kt_mla, v_mla, n_batch=nb, seq=seq)

    xs = x_sample.reshape(ms, d)
    cos_s, sin_s = _rope_tables(jnp.tile(past + jnp.arange(dec, dtype=jnp.int32), ROW_TILE // dec))
    sbq_s, sbk_s, sbv_s = _sb_proj(xs, w_sb)
    q_cat_s, ckv_s, kr_s = _mla_q_proj(xs, w_lat, q_a_norm, kv_a_norm, w_uq_all, cos_s, sin_s)
    gates_s = _gates(xs, w_g)
    o_sb_s = _sb_decode(sbq_s, sbk_s, sbv_s, cache_sb_k.reshape(nd, past * SB_HEADS, SB_HEAD_DIM),
                        cache_sb_v.reshape(nd, past * SB_HEADS, SB_HEAD_DIM), dec=dec)
    o_mla_s = _mla_decode(q_cat_s, w_uk, w_uv, cache_mla_ckv[0], cache_mla_krope[0], ckv_s, kr_s,
                          q_blk0=0, dec=dec)

    x1 = _merge(xp, o_sb_p, o_mla_p, gates_p, w_a, w_b, w_out, ln1_g, ln1_b, alpha=alpha, out_rows=mp + ms)
    x1 = _merge(xs, o_sb_s, o_mla_s, gates_s, w_a, w_b, w_out, ln1_g, ln1_b, alpha=alpha, out_rows=mp + ms,
                row0=mp, into=x1)
    route = _router(x1, w_r_hi, w_r_lo)
    tile_expert, n_used, w_slot, next_expert, row_token, row_w, pos1, pos2 = _route_tables(route, ROW_TILE)
    y_sorted = _moe(tile_expert, n_used, w_slot, next_expert, row_token, x1, w_exp_gate[0], w_exp_up[0],
                    w_exp_down[0], row_w, tm=ROW_TILE)
    y_p = _final(pos1[:mp], pos2[:mp], y_sorted, x1, 0, p_prompt[0].reshape(mp, -1), w_pp, w_pg,
                 ln2_g, ln2_b, alpha=alpha)
    y_s = _final(pos1[mp:], pos2[mp:], y_sorted, x1, mp // ROW_TILE, p_sample[0].reshape(ms, -1), w_pp, w_pg,
                 ln2_g, ln2_b, alpha=alpha)

    hs = (SB_HEADS, SB_HEAD_DIM)
    return (y_p.reshape(nb, seq, d), y_s.reshape(nd, dec, d),
            sbk_p.reshape((1, nb, seq) + hs), sbv_p.reshape((1, nb, seq) + hs),
            ckv_p.reshape(1, nb, seq, kv_lora), kr_p.reshape(1, nb, seq, ROPE_DIM),
            sbk_s.reshape((1, nd, dec) + hs), sbv_s.reshape((1, nd, dec) + hs),
            ckv_s.reshape(1, nd, dec, kv_lora), kr_s.reshape(1, nd, dec, ROPE_DIM))
```
